```python
import math
import jax
import jax.numpy as jnp
from jax import lax
import numpy as np

D_MODEL = 2048
BATCH = 4
SEQ = 2048
DEPTH = 4
DEC_BATCH = 8
DEC_SEQ = 4
PAST_LEN = 16384
PAGE_SIZE = 128

H_A = 8
DK_A = 128
DV_A = 128
H_B = 8
HD_B = 128
SB_BIAS_MIN = 2.0
SB_BIAS_MAX = 12.0
H_C = 8
DK_C = 128
DV_C = 128
CONV_K = 4
CONV_C = H_C * (2 * DK_C + DV_C)
N_BRANCH = 3
BR_W = 1024
D_FF = 5632
CHUNK = 64
Q_BLOCK = 128
RMS_EPS = 1e-6
SPLIT_SIZES = (H_A * DK_A, H_A * DK_A, H_A * DV_A, H_A * DV_A,
               H_B * HD_B, H_B * HD_B, H_B * HD_B,
               CONV_C, H_C * DV_C, H_C, H_C,
               N_BRANCH * D_MODEL)
IN_W = sum(SPLIT_SIZES)

kernel_name = 'hgrn2_stickbreak_gdn_macaron_step'


def _rms(x, g):
    xf = x.astype(jnp.float32)
    y = xf * lax.rsqrt(jnp.mean(xf * xf, axis=-1, keepdims=True) + RMS_EPS)
    return (y * g.astype(jnp.float32)).astype(x.dtype)


def _l2norm(x):
    xf = x.astype(jnp.float32)
    return xf * lax.rsqrt(jnp.sum(xf * xf, axis=-1, keepdims=True) + 1e-6)


def _swiglu(x, wg, wu, wd):
    return (jax.nn.silu(x @ wg) * (x @ wu)) @ wd


def _causal_dwconv(xx, w):
    return lax.conv_general_dilated(xx, w.astype(xx.dtype)[:, None, :], window_strides=(1,),
                                    padding='VALID', dimension_numbers=('NWC', 'WIO', 'NWC'),
                                    feature_group_count=xx.shape[-1])


def _hgrn2_chunked(q, k, v, log_f, s0):
    B, T, H, _ = q.shape
    c = math.gcd(T, CHUNK)
    n = T // c

    def blocks(a):
        return a.reshape(B, n, c, H, a.shape[-1]).transpose(1, 0, 3, 2, 4)

    qc, kc, vc = blocks(q), blocks(k), blocks(v)
    bc = jnp.cumsum(blocks(log_f), axis=3)
    incl = jnp.tril(jnp.ones((c, c), dtype=bool))[:, :, None]

    def step(S, xs):
        qi, ki, vi, bi = xs
        o_inter = jnp.einsum('bhtk,bhkv->bhtv', qi * jnp.exp(bi), S)
        dec = jnp.exp(jnp.where(incl, bi[:, :, :, None, :] - bi[:, :, None, :, :], -jnp.inf))
        attn = jnp.einsum('bhtk,bhtsk,bhsk->bhts', qi, dec, ki)
        o = o_inter + jnp.einsum('bhts,bhsv->bhtv', attn, vi)
        blast = bi[:, :, -1:, :]
        S = S * jnp.exp(blast[:, :, 0, :, None]) + jnp.einsum('bhsk,bhsv->bhkv', ki * jnp.exp(blast - bi), vi)
        return S, o

    S, o = lax.scan(step, s0, (qc, kc, vc, bc))
    return o.transpose(1, 0, 3, 2, 4).reshape(B, T, H, -1), S


def _gated_delta_chunked(q, k, v, log_a, beta, s0):
    B, T, H, _ = q.shape
    c = math.gcd(T, CHUNK)
    n = T // c

    def blocks(a):
        return a.reshape(B, n, c, H, a.shape[-1]).transpose(1, 0, 3, 2, 4)

    qc, kc, vc = blocks(q), blocks(k), blocks(v)
    g = jnp.cumsum(log_a.reshape(B, n, c, H).transpose(1, 0, 3, 2), axis=-1)
    bt = beta.reshape(B, n, c, H).transpose(1, 0, 3, 2)[..., None]
    incl = jnp.tril(jnp.ones((c, c), dtype=bool))
    strict = jnp.tril(jnp.ones((c, c), dtype=bool), -1)
    gam = jnp.exp(jnp.where(incl, g[..., :, None] - g[..., None, :], -jnp.inf))
    kb = kc * bt
    lmat = jnp.where(strict, jnp.einsum('nbhtk,nbhsk->nbhts', kb, kc) * gam, 0.0) + jnp.eye(c, dtype=q.dtype)
    u = lax.linalg.triangular_solve(lmat, vc * bt, left_side=True, lower=True, unit_diagonal=True)
    w = lax.linalg.triangular_solve(lmat, kb * jnp.exp(g)[..., None], left_side=True, lower=True,
                                    unit_diagonal=True)
    aqk = jnp.einsum('nbhtk,nbhsk->nbhts', qc, kc) * gam

    def step(S, xs):
        qi, ki, ui, wi, gi, ai = xs
        v_new = ui - jnp.einsum('bhtk,bhkv->bhtv', wi, S)
        o = jnp.einsum('bhtk,bhkv->bhtv', qi * jnp.exp(gi)[..., None], S) + jnp.einsum('bhts,bhsv->bhtv', ai, v_new)
        glast = gi[..., -1]
        S = S * jnp.exp(glast)[..., None, None] + jnp.einsum(
            'bhsk,bhsv->bhkv', ki * jnp.exp(glast[..., None] - gi)[..., None], v_new)
        return S, o

    S, o = lax.scan(step, s0, (qc, kc, u, w, g, aqk))
    return o.transpose(1, 0, 3, 2, 4).reshape(B, T, H, -1), S


def _stick_breaking(q, k, v, bias, q_offset):
    B, T, H, hd = q.shape
    qb = math.gcd(T, Q_BLOCK)
    nb = T // qb
    kf = k.astype(jnp.float32)
    vf = v.astype(jnp.float32)
    bf = bias.astype(jnp.float32)[None, :, None, None]
    kpos = jnp.arange(k.shape[1])
    q_blocks = q.astype(jnp.float32).reshape(B, nb, qb, H, hd).transpose(1, 0, 2, 3, 4)

    def one_block(args):
        qblk, start = args
        z = jnp.einsum('bqhd,bkhd->bhqk', qblk, kf) * (hd ** -0.5) + bf
        qpos = q_offset + start + jnp.arange(qb)
        causal = kpos[None, :] < qpos[:, None]
        log_fail = jnp.where(causal, jax.nn.log_sigmoid(-z), 0.0)
        log_after = lax.cumsum(log_fail, axis=3, reverse=True) - log_fail
        w = jnp.where(causal, jnp.exp(jax.nn.log_sigmoid(z) + log_after), 0.0)
        return jnp.einsum('bhqk,bkhd->bqhd', w, vf)

    o = lax.map(one_block, (q_blocks, jnp.arange(nb) * qb))
    return o.transpose(1, 0, 2, 3, 4).reshape(B, T, H, hd)


def _token_mix(u, lb, st_a, st_c, conv_buf, past_k, past_v,
               w_in, conv_w, a_log, dt_bias, norm_a, norm_c, sb_bias, w_branch, w_out):
    B, T, _ = u.shape
    f32 = jnp.float32
    proj = u @ w_in
    cuts = np.cumsum(SPLIT_SIZES)[:-1].tolist()
    (a_q, a_f, a_i, a_g, b_q, b_k, b_v, c_qkv, c_z, c_a, c_b, br_g) = jnp.split(proj, cuts, axis=-1)

    lbh = lb.reshape(H_A, DK_A)
    log_f = jnp.logaddexp(jnp.log(lbh), jnp.log1p(-lbh) + jax.nn.log_sigmoid(
        a_f.astype(f32).reshape(B, T, H_A, DK_A)))
    q_a = jax.nn.silu(a_q.astype(f32).reshape(B, T, H_A, DK_A)) * (DK_A ** -0.5)
    o_a, st_a = _hgrn2_chunked(q_a, -jnp.expm1(log_f), a_i.astype(f32).reshape(B, T, H_A, DV_A),
                               log_f, st_a.astype(f32))
    o_a = _rms(o_a, norm_a) * jax.nn.silu(a_g.astype(f32).reshape(B, T, H_A, DV_A))

    k_new = b_k.reshape(B, T, H_B, HD_B)
    v_new = b_v.reshape(B, T, H_B, HD_B)
    k_all = jnp.concatenate([past_k.astype(k_new.dtype), k_new], axis=1)
    v_all = jnp.concatenate([past_v.astype(v_new.dtype), v_new], axis=1)
    o_b = _stick_breaking(b_q.reshape(B, T, H_B, HD_B), k_all, v_all, sb_bias, past_k.shape[1])

    conv_in = jnp.concatenate([conv_buf.astype(c_qkv.dtype), c_qkv], axis=1)
    conv_state = conv_in[:, conv_in.shape[1] - (CONV_K - 1):]
    qkv = jax.nn.silu(_causal_dwconv(conv_in, conv_w))
    c_q, c_k, c_v = jnp.split(qkv, [H_C * DK_C, 2 * H_C * DK_C], axis=-1)
    q_c = _l2norm(c_q.reshape(B, T, H_C, DK_C)) * (DK_C ** -0.5)
    k_c = _l2norm(c_k.reshape(B, T, H_C, DK_C))
    log_a = -jnp.exp(a_log.astype(f32)) * jax.nn.softplus(c_a.astype(f32) + dt_bias.astype(f32))
    beta = jax.nn.sigmoid(c_b.astype(f32))
    o_c, st_c = _gated_delta_chunked(q_c, k_c, c_v.astype(f32).reshape(B, T, H_C, DV_C), log_a, beta,
                                     st_c.astype(f32))
    o_c = _rms(o_c, norm_c) * jax.nn.silu(c_z.astype(f32).reshape(B, T, H_C, DV_C))

    branches = jnp.stack([o_a.reshape(B, T, BR_W), o_b.reshape(B, T, BR_W), o_c.reshape(B, T, BR_W)],
                         axis=2).astype(u.dtype)
    gates = jax.nn.sigmoid(br_g.reshape(B, T, N_BRANCH, D_MODEL))
    merged = jnp.einsum('btnc,ncd->btnd', branches, w_branch)
    y = jnp.sum(merged * gates, axis=2) @ w_out
    return y, st_a, st_c, conv_state, k_new, v_new


def _layer(x, lb, st_a, st_c, conv_buf, past_k, past_v,
           f1_pre, f1_post, f1_wg, f1_wu, f1_wd,
           m_pre, m_post, w_in, conv_w, a_log, dt_bias, norm_a, norm_c, sb_bias, w_branch, w_out,
           f2_pre, f2_post, f2_wg, f2_wu, f2_wd):
    h = x + 0.5 * _rms(_swiglu(_rms(x, f1_pre), f1_wg, f1_wu, f1_wd), f1_post)
    m, st_a, st_c, conv_buf, k_new, v_new = _token_mix(
        _rms(h, m_pre), lb, st_a, st_c, conv_buf, past_k, past_v,
        w_in, conv_w, a_log, dt_bias, norm_a, norm_c, sb_bias, w_branch, w_out)
    h = h + _rms(m, m_post)
    h = h + 0.5 * _rms(_swiglu(_rms(h, f2_pre), f2_wg, f2_wu, f2_wd), f2_post)
    return h, st_a, st_c, conv_buf, k_new, v_new


def setup_inputs(seed: int = 0) -> dict:
    key = jax.random.key(seed)
    ks = iter(jax.random.split(key, 48))
    f32 = jnp.float32
    n_pages = PAST_LEN // PAGE_SIZE
    n_used = DEC_BATCH * n_pages
    n_phys = n_used + max(1, n_used // 4)

    def nrm(shape, scale):
        return jax.random.normal(next(ks), shape, f32) * scale

    def gain(shape):
        return 1.0 + nrm(shape, 0.02)

    x_prompt = nrm((BATCH, SEQ, D_MODEL), 1.0)
    x_sample = nrm((DEC_BATCH, DEC_SEQ, D_MODEL), 1.0)
    cache_k = nrm((DEPTH, n_phys, PAGE_SIZE, H_B, HD_B), 1.0)
    cache_v = nrm((DEPTH, n_phys, PAGE_SIZE, H_B, HD_B), 1.0)
    state_hgrn = nrm((DEPTH, DEC_BATCH, H_A, DK_A, DV_A), 0.3)
    state_gdn = nrm((DEPTH, DEC_BATCH, H_C, DK_C, DV_C), 0.1)
    state_conv = nrm((DEPTH, DEC_BATCH, CONV_K - 1, CONV_C), 1.0)
    page_table = jax.random.permutation(next(ks), n_phys)[:n_used].reshape(DEC_BATCH, n_pages).astype(jnp.int32)

    dt = jnp.exp(jax.random.uniform(next(ks), (DEPTH, H_C), f32, math.log(1e-3), math.log(1e-1)))
    sb_bias = -jnp.linspace(SB_BIAS_MIN, SB_BIAS_MAX, H_B, dtype=f32)[None, :] + nrm((DEPTH, H_B), 0.1)
    return {
        'x_prompt': x_prompt,
        'x_sample': x_sample,
        'cache_k': cache_k,
        'cache_v': cache_v,
        'state_hgrn': state_hgrn,
        'state_gdn': state_gdn,
        'state_conv': state_conv,
        'page_table': page_table,
        'ffn1_norm_pre': gain((DEPTH, D_MODEL)),
        'ffn1_norm_post': gain((DEPTH, D_MODEL)),
        'ffn1_w_gate': nrm((DEPTH, D_MODEL, D_FF), D_MODEL ** -0.5),
        'ffn1_w_up': nrm((DEPTH, D_MODEL, D_FF), D_MODEL ** -0.5),
        'ffn1_w_down': nrm((DEPTH, D_FF, D_MODEL), D_FF ** -0.5),
        'mix_norm_pre': gain((DEPTH, D_MODEL)),
        'mix_norm_post': gain((DEPTH, D_MODEL)),
        'w_in': nrm((DEPTH, D_MODEL, IN_W), D_MODEL ** -0.5),
        'hgrn_lb_raw': nrm((DEPTH, H_A * DK_A), 0.1),
        'hgrn_norm': gain((DEPTH, DV_A)),
        'gdn_conv_w': nrm((DEPTH, CONV_K, CONV_C), CONV_K ** -0.5),
        'gdn_a_log': jnp.log(jax.random.uniform(next(ks), (DEPTH, H_C), f32, 1.0, 16.0)),
        'gdn_dt_bias': dt + jnp.log(-jnp.expm1(-dt)),
        'gdn_norm': gain((DEPTH, DV_C)),
        'sb_bias': sb_bias,
        'w_branch': nrm((DEPTH, N_BRANCH, BR_W, D_MODEL), BR_W ** -0.5),
        'w_out': nrm((DEPTH, D_MODEL, D_MODEL), D_MODEL ** -0.5),
        'ffn2_norm_pre': gain((DEPTH, D_MODEL)),
        'ffn2_norm_post': gain((DEPTH, D_MODEL)),
        'ffn2_w_gate': nrm((DEPTH, D_MODEL, D_FF), D_MODEL ** -0.5),
        'ffn2_w_up': nrm((DEPTH, D_MODEL, D_FF), D_MODEL ** -0.5),
        'ffn2_w_down': nrm((DEPTH, D_FF, D_MODEL), D_FF ** -0.5),
    }


def reference(x_prompt, x_sample, cache_k, cache_v, state_hgrn, state_gdn, state_conv, page_table,
              ffn1_norm_pre, ffn1_norm_post, ffn1_w_gate, ffn1_w_up, ffn1_w_down,
              mix_norm_pre, mix_norm_post, w_in, hgrn_lb_raw, hgrn_norm,
              gdn_conv_w, gdn_a_log, gdn_dt_bias, gdn_norm, sb_bias, w_branch, w_out,
              ffn2_norm_pre, ffn2_norm_post, ffn2_w_gate, ffn2_w_up, ffn2_w_down):
    f32 = jnp.float32
    lbs = jnp.cumsum(jax.nn.softmax(hgrn_lb_raw.astype(f32), axis=0), axis=0)
    lbs = lbs - lbs[:1]
    bp = x_prompt.shape[0]
    db, n_pages = page_table.shape
    ps = cache_k.shape[2]
    zero_a = jnp.zeros((bp, H_A, DK_A, DV_A), f32)
    zero_c = jnp.zeros((bp, H_C, DK_C, DV_C), f32)
    zero_conv = jnp.zeros((bp, CONV_K - 1, CONV_C), x_prompt.dtype)
    zero_kv = jnp.zeros((bp, 0, H_B, HD_B), x_prompt.dtype)

    yp, ys = x_prompt, x_sample
    hp, hs, gp, gs, cp, cs, kp, vp, ksm, vsm = [], [], [], [], [], [], [], [], [], []
    for l in range(DEPTH):
        w_l = (ffn1_norm_pre[l], ffn1_norm_post[l], ffn1_w_gate[l], ffn1_w_up[l], ffn1_w_down[l],
               mix_norm_pre[l], mix_norm_post[l], w_in[l], gdn_conv_w[l], gdn_a_log[l], gdn_dt_bias[l],
               hgrn_norm[l], gdn_norm[l], sb_bias[l], w_branch[l], w_out[l],
               ffn2_norm_pre[l], ffn2_norm_post[l], ffn2_w_gate[l], ffn2_w_up[l], ffn2_w_down[l])
        yp, sa, sc, cv, kn, vn = _layer(yp, lbs[l], zero_a, zero_c, zero_conv, zero_kv, zero_kv, *w_l)
        hp.append(sa); gp.append(sc); cp.append(cv); kp.append(kn); vp.append(vn)
        past_k = cache_k[l][page_table].reshape(db, n_pages * ps, H_B, HD_B)
        past_v = cache_v[l][page_table].reshape(db, n_pages * ps, H_B, HD_B)
        ys, sa, sc, cv, kn, vn = _layer(ys, lbs[l], state_hgrn[l], state_gdn[l], state_conv[l],
                                        past_k, past_v, *w_l)
        hs.append(sa); gs.append(sc); cs.append(cv); ksm.append(kn); vsm.append(vn)

    hgrn_prompt = jnp.stack(hp)
    hgrn_sample = jnp.stack(hs)
    gdn_prompt = jnp.stack(gp)
    gdn_sample = jnp.stack(gs)
    conv_prompt = jnp.stack(cp)
    conv_sample = jnp.stack(cs)
    k_prompt = jnp.stack(kp)
    v_prompt = jnp.stack(vp)
    k_sample = jnp.stack(ksm)
    v_sample = jnp.stack(vsm)
    return (yp, ys, hgrn_prompt, hgrn_sample, gdn_prompt, gdn_sample, conv_prompt, conv_sample,
            k_prompt, v_prompt, k_sample, v_sample)
```

```python
import functools
import math

import jax
import jax.numpy as jnp
from jax import lax
from jax.experimental import pallas as pl
from jax.experimental.pallas import tpu as pltpu

F32 = jnp.float32
BF16 = jnp.bfloat16

D_MODEL = 2048
D_FF = 5632
N_HEADS = 8
HEAD_DIM = 128
BR_W = N_HEADS * HEAD_DIM
CONV_K = 4
RMS_EPS = 1e-6
LANES = 128
SUBLANES = 8
VMEM_LIMIT = 56 * 1024 * 1024

GATE_BLK = 0
AQ_BLK, AF_BLK, AI_BLK, AG_BLK = 48, 56, 64, 72
BQ_BLK, BK_BLK, BV_BLK = 80, 88, 96
CQ_BLK, CK_BLK, CV_BLK = 104, 112, 120
CZ_BLK = 128
AB_BLK = 136
PROJ_W = 140 * LANES

HGRN_CHUNK = 32
GDN_CHUNK = 64
SB_BLOCK = 256
PAGE = 128


def _cparams(semantics):
    return pltpu.CompilerParams(dimension_semantics=semantics, vmem_limit_bytes=VMEM_LIMIT)


def _rms(x, g):
    ms = jnp.mean(x * x, axis=-1, keepdims=True)
    return x * lax.rsqrt(ms + RMS_EPS) * g


def _silu(x):
    return x * jax.nn.sigmoid(x)


def _dot(a, b):
    return jnp.dot(a.astype(BF16), b.astype(BF16), preferred_element_type=F32)


def _dot_nt(a, b):
    return lax.dot_general(a.astype(BF16), b.astype(BF16), (((1,), (1,)), ((), ())),
                           preferred_element_type=F32)


def _dot_tn(a, b):
    return lax.dot_general(a.astype(BF16), b.astype(BF16), (((0,), (0,)), ((), ())),
                           preferred_element_type=F32)


def _split3(x):
    hi = x.astype(BF16)
    r1 = x - hi.astype(F32)
    mid = r1.astype(BF16)
    lo = (r1 - mid.astype(F32)).astype(BF16)
    return hi, mid, lo


def _dot_exact_lhs(m_bf16, x):
    hi, mid, lo = _split3(x)
    d = functools.partial(jnp.dot, preferred_element_type=F32)
    return d(m_bf16, hi) + d(m_bf16, mid) + d(m_bf16, lo)


def _dot_exact_rhs(x, m_bf16):
    hi, mid, lo = _split3(x)
    d = functools.partial(jnp.dot, preferred_element_type=F32)
    return d(hi, m_bf16) + d(mid, m_bf16) + d(lo, m_bf16)


def _iota2(shape, dim):
    return lax.broadcasted_iota(jnp.int32, shape, dim)


def _ffn_body(x_ref, pre_ref, post_ref, wg_ref, wu_ref, wd_ref, o_ref, xn_ref, acc_ref):
    j = pl.program_id(1)

    @pl.when(j == 0)
    def _():
        xn_ref[...] = _rms(x_ref[...], pre_ref[...]).astype(BF16)
        acc_ref[...] = jnp.zeros_like(acc_ref)

    xn = xn_ref[...]
    g = jnp.dot(xn, wg_ref[...], preferred_element_type=F32)
    u = jnp.dot(xn, wu_ref[...], preferred_element_type=F32)
    acc_ref[...] += _dot(_silu(g) * u, wd_ref[...])

    @pl.when(j == pl.num_programs(1) - 1)
    def _():
        o_ref[...] = x_ref[...] + 0.5 * _rms(acc_ref[...], post_ref[...])


def _ffn(x, pre, post, wg, wu, wd, *, tm, tf):
    m = x.shape[0]
    return pl.pallas_call(
        _ffn_body,
        grid=(m // tm, D_FF // tf),
        in_specs=[
            pl.BlockSpec((tm, D_MODEL), lambda i, j: (i, 0)),
            pl.BlockSpec((1, D_MODEL), lambda i, j: (0, 0)),
            pl.BlockSpec((1, D_MODEL), lambda i, j: (0, 0)),
            pl.BlockSpec((D_MODEL, tf), lambda i, j: (0, j)),
            pl.BlockSpec((D_MODEL, tf), lambda i, j: (0, j)),
            pl.BlockSpec((tf, D_MODEL), lambda i, j: (j, 0)),
        ],
        out_specs=pl.BlockSpec((tm, D_MODEL), lambda i, j: (i, 0)),
        out_shape=jax.ShapeDtypeStruct((m, D_MODEL), F32),
        scratch_shapes=[pltpu.VMEM((tm, D_MODEL), BF16), pltpu.VMEM((tm, D_MODEL), F32)],
        compiler_params=_cparams(("parallel", "arbitrary")),
        name="ffn",
    )(x, pre, post, wg, wu, wd)


def _inproj_body(x_ref, pre_ref, w_ref, o_ref, xn_ref):
    @pl.when(pl.program_id(1) == 0)
    def _():
        xn_ref[...] = _rms(x_ref[...], pre_ref[...]).astype(BF16)

    o_ref[...] = jnp.dot(xn_ref[...], w_ref[...], preferred_element_type=F32)


def _inproj(x, pre, w, *, tm, tn):
    m = x.shape[0]
    return pl.pallas_call(
        _inproj_body,
        grid=(m // tm, PROJ_W // tn),
        in_specs=[
            pl.BlockSpec((tm, D_MODEL), lambda i, j: (i, 0)),
            pl.BlockSpec((1, D_MODEL), lambda i, j: (0, 0)),
            pl.BlockSpec((D_MODEL, tn), lambda i, j: (0, j)),
        ],
        out_specs=pl.BlockSpec((tm, tn), lambda i, j: (i, j)),
        out_shape=jax.ShapeDtypeStruct((m, PROJ_W), F32),
        scratch_shapes=[pltpu.VMEM((tm, D_MODEL), BF16)],
        compiler_params=_cparams(("parallel", "arbitrary")),
        name="inproj",
    )(x, pre, w)


def _merge_body(oa_ref, ob_ref, oc_ref, gate_ref, w_ref, post_ref, h_ref, out_ref, acc_ref, y_ref):
    j = pl.program_id(1)

    def branch(o_ref, first):
        m = jnp.dot(o_ref[...], w_ref[...], preferred_element_type=F32) * jax.nn.sigmoid(gate_ref[...])
        if first:
            acc_ref[...] = m
        else:
            acc_ref[...] += m

    pl.when(j == 0)(lambda: branch(oa_ref, True))
    pl.when(j == 1)(lambda: branch(ob_ref, False))
    pl.when(j == 2)(lambda: branch(oc_ref, False))

    @pl.when(j == 3)
    def _():
        y_ref[...] = _dot(acc_ref[:, :BR_W], w_ref[...])

    @pl.when(j == 4)
    def _():
        y = y_ref[...] + _dot(acc_ref[:, BR_W:], w_ref[...])
        out_ref[...] = h_ref[...] + _rms(y, post_ref[...])


def _merge(oa, ob, oc, proj, w_cat, post, h, *, tm):
    m = h.shape[0]
    o_spec = pl.BlockSpec((tm, BR_W), lambda i, j: (i, 0))
    return pl.pallas_call(
        _merge_body,
        grid=(m // tm, 5),
        in_specs=[
            o_spec, o_spec, o_spec,
            pl.BlockSpec((tm, D_MODEL), lambda i, j: (i, jnp.minimum(j, 2))),
            pl.BlockSpec((BR_W, D_MODEL), lambda i, j: (j, 0)),
            pl.BlockSpec((1, D_MODEL), lambda i, j: (0, 0)),
            pl.BlockSpec((tm, D_MODEL), lambda i, j: (i, 0)),
        ],
        out_specs=pl.BlockSpec((tm, D_MODEL), lambda i, j: (i, 0)),
        out_shape=jax.ShapeDtypeStruct((m, D_MODEL), F32),
        scratch_shapes=[pltpu.VMEM((tm, D_MODEL), F32), pltpu.VMEM((tm, D_MODEL), F32)],
        compiler_params=_cparams(("parallel", "arbitrary")),
        name="merge",
    )(oa, ob, oc, proj, w_cat, post, h)


def _hgrn_body(q_ref, f_ref, i_ref, g_ref, lb_ref, norm_ref, s0_ref, o_ref, sout_ref, st_ref,
               *, tb, chunk, t_valid):
    tt = pl.program_id(2)

    @pl.when(tt == 0)
    def _():
        st_ref[...] = s0_ref[...].T

    lb = lb_ref[...]
    one_m_lb = 1.0 - lb
    row = _iota2((chunk, chunk), 0)
    col = _iota2((chunk, chunk), 1)
    incl = row >= col
    tril = incl.astype(BF16)
    scale = HEAD_DIM ** -0.5

    def chunk_step(ci, carry):
        r0 = pl.multiple_of(ci * chunk, chunk)
        rows = pl.ds(r0, chunk)
        sig = jax.nn.sigmoid(f_ref[rows, :])
        log_f = jnp.log(lb + one_m_lb * sig)
        k = one_m_lb * (1.0 - sig)
        q = _silu(q_ref[rows, :]) * scale
        v = i_ref[rows, :]
        if t_valid < tb:
            ok = (_iota2((chunk, 1), 0) + r0) < t_valid
            log_f = jnp.where(ok, log_f, 0.0)
            k = jnp.where(ok, k, 0.0)
        b = _dot_exact_lhs(tril, log_f)
        qh = q * jnp.exp(b)
        kh = k * jnp.exp(-b)
        st = st_ref[...]
        attn = jnp.where(incl, _dot_nt(qh, kh), 0.0)
        o = _dot_nt(qh, st) + _dot(attn, v)
        b_last = b[chunk - 1:chunk, :]
        st_ref[...] = st * jnp.exp(b_last) + _dot_tn(v, k * jnp.exp(b_last - b))
        o_ref[rows, :] = (_rms(o, norm_ref[...]) * _silu(g_ref[rows, :])).astype(o_ref.dtype)
        return carry

    lax.fori_loop(0, tb // chunk, chunk_step, 0)

    @pl.when(tt == pl.num_programs(2) - 1)
    def _():
        sout_ref[...] = st_ref[...].T


def _hgrn(proj, lb, norm, s0, *, tb, t_valid):
    bsz, t, _ = proj.shape
    chunk = min(HGRN_CHUNK, tb)

    def pspec(blk):
        return pl.BlockSpec((None, tb, LANES), lambda b, h, tt: (b, tt, blk + h))

    st_spec = pl.BlockSpec((None, None, HEAD_DIM, HEAD_DIM), lambda b, h, tt: (b, h, 0, 0))
    return pl.pallas_call(
        functools.partial(_hgrn_body, tb=tb, chunk=chunk, t_valid=t_valid),
        grid=(bsz, N_HEADS, t // tb),
        in_specs=[
            pspec(AQ_BLK), pspec(AF_BLK), pspec(AI_BLK), pspec(AG_BLK),
            pl.BlockSpec((1, LANES), lambda b, h, tt: (0, h)),
            pl.BlockSpec((1, LANES), lambda b, h, tt: (0, 0)),
            st_spec,
        ],
        out_specs=[
            pl.BlockSpec((None, tb, LANES), lambda b, h, tt: (b, tt, h)),
            st_spec,
        ],
        out_shape=[
            jax.ShapeDtypeStruct((bsz, t, BR_W), BF16),
            jax.ShapeDtypeStruct((bsz, N_HEADS, HEAD_DIM, HEAD_DIM), F32),
        ],
        scratch_shapes=[pltpu.VMEM((HEAD_DIM, HEAD_DIM), F32)],
        compiler_params=_cparams(("parallel", "parallel", "arbitrary")),
        name="hgrn",
    )(proj, proj, proj, proj, lb, norm, s0)


def _unit_lower_inverse(a, n):
    row = _iota2((n, n), 0)
    col = _iota2((n, n), 1)
    eye = (row == col).astype(F32)
    t = eye - jnp.where((row // 2 == col // 2), a, 0.0)
    s = 2
    while s < n:
        below = (row // (2 * s) == col // (2 * s)) & (row // s > col // s)
        t = t - _dot(t, _dot(jnp.where(below, a, 0.0), t))
        s *= 2
    return t


def _gdn_body(q_ref, k_ref, v_ref, z_ref, ab_ref, wq_ref, wk_ref, wv_ref, cq_ref, ck_ref, cv_ref,
              alog_ref, dt_ref, norm_ref, s0_ref, o_ref, sout_ref,
              s_ref, xq_ref, xk_ref, xv_ref, *, tb, chunk, t_valid):
    h = pl.program_id(1)
    tt = pl.program_id(2)
    pad = SUBLANES
    keep = CONV_K - 1

    @pl.when(tt == 0)
    def _():
        s_ref[...] = s0_ref[...]
        xq_ref[pad - keep:pad, :] = cq_ref[...]
        xk_ref[pad - keep:pad, :] = ck_ref[...]
        xv_ref[pad - keep:pad, :] = cv_ref[...]

    def conv(x_ref, src_ref, w_ref):
        x_ref[pad:pad + tb, :] = src_ref[...]
        w = w_ref[...]
        y = x_ref[pad:pad + tb, :] * w[CONV_K - 1:CONV_K, :]
        for j in range(keep):
            y = y + x_ref[pad - keep + j:pad - keep + j + tb, :] * w[j:j + 1, :]
        x_ref[pad - keep:pad, :] = x_ref[pad + tb - keep:pad + tb, :]
        return _silu(y)

    xq_ref[pad:pad + tb, :] = conv(xq_ref, q_ref, wq_ref)
    xk_ref[pad:pad + tb, :] = conv(xk_ref, k_ref, wk_ref)
    xv_ref[pad:pad + tb, :] = conv(xv_ref, v_ref, wv_ref)

    lane = _iota2((1, LANES), 1)
    neg_a = -jnp.exp(alog_ref[...])
    row = _iota2((chunk, chunk), 0)
    col = _iota2((chunk, chunk), 1)
    incl = row >= col
    strict = row > col
    tril = incl.astype(BF16)
    triu = (row <= col).astype(BF16)

    def chunk_step(ci, carry):
        r0 = pl.multiple_of(ci * chunk, chunk)
        rows = pl.ds(r0, chunk)
        xrows = pl.ds(pad + r0, chunk)
        qc = xq_ref[xrows, :]
        kc = xk_ref[xrows, :]
        vc = xv_ref[xrows, :]
        qn = qc * lax.rsqrt(jnp.sum(qc * qc, axis=-1, keepdims=True) + 1e-6) * (HEAD_DIM ** -0.5)
        kn = kc * lax.rsqrt(jnp.sum(kc * kc, axis=-1, keepdims=True) + 1e-6)
        ab = ab_ref[rows, :]
        log_a_all = neg_a * jax.nn.softplus(ab + dt_ref[...])
        log_a = jnp.sum(jnp.where(lane == h, log_a_all, 0.0), axis=-1, keepdims=True)
        beta = jax.nn.sigmoid(jnp.sum(jnp.where(lane == N_HEADS + h, ab, 0.0), axis=-1, keepdims=True))
        if t_valid < tb:
            ok = (_iota2((chunk, 1), 0) + r0) < t_valid
            log_a = jnp.where(ok, log_a, 0.0)
            beta = jnp.where(ok, beta, 0.0)
        la_b = jnp.broadcast_to(log_a, (chunk, chunk))
        g_col = _dot_exact_lhs(tril, la_b)
        hi, mid, lo = _split3(la_b)
        dtn = functools.partial(lax.dot_general, dimension_numbers=(((0,), (0,)), ((), ())),
                                preferred_element_type=F32)
        g_row = dtn(hi, triu) + dtn(mid, triu) + dtn(lo, triu)
        g = g_col[:, 0:1]
        gam = jnp.where(incl, jnp.exp(g_col - g_row), 0.0)
        kk = _dot_nt(kn, kn)
        a = jnp.where(strict, kk * gam * beta, 0.0)
        t_inv = _unit_lower_inverse(a, chunk)
        u = _dot(t_inv, vc * beta)
        w = _dot(t_inv, kn * (beta * jnp.exp(g)))
        aqk = _dot_nt(qn, kn) * gam
        s = s_ref[...]
        v_new = u - _dot(w, s)
        o = _dot(qn * jnp.exp(g), s) + _dot(aqk, v_new)
        g_last = g_col[chunk - 1:chunk, 0:1]
        s_ref[...] = s * jnp.exp(g_last) + _dot_tn(kn * jnp.exp(g_last - g), v_new)
        o_ref[rows, :] = (_rms(o, norm_ref[...]) * _silu(z_ref[rows, :])).astype(o_ref.dtype)
        return carry

    lax.fori_loop(0, tb // chunk, chunk_step, 0)

    @pl.when(tt == pl.num_programs(2) - 1)
    def _():
        sout_ref[...] = s_ref[...]


def _gdn(proj, conv_w, conv0, alog, dtb, norm, s0, *, tb, t_valid):
    bsz, t, _ = proj.shape
    chunk = min(GDN_CHUNK, tb)
    keep = CONV_K - 1

    def pspec(blk):
        return pl.BlockSpec((None, tb, LANES), lambda b, h, tt: (b, tt, blk + h))

    def wspec(blk):
        return pl.BlockSpec((CONV_K, LANES), lambda b, h, tt: (0, blk + h))

    def cspec(blk):
        return pl.BlockSpec((None, keep, LANES), lambda b, h, tt: (b, 0, blk + h))

    row_spec = pl.BlockSpec((1, LANES), lambda b, h, tt: (0, 0))
    st_spec = pl.BlockSpec((None, None, HEAD_DIM, HEAD_DIM), lambda b, h, tt: (b, h, 0, 0))
    return pl.pallas_call(
        functools.partial(_gdn_body, tb=tb, chunk=chunk, t_valid=t_valid),
        grid=(bsz, N_HEADS, t // tb),
        in_specs=[
            pspec(CQ_BLK), pspec(CK_BLK), pspec(CV_BLK), pspec(CZ_BLK),
            pl.BlockSpec((None, tb, LANES), lambda b, h, tt: (b, tt, AB_BLK)),
            wspec(0), wspec(N_HEADS), wspec(2 * N_HEADS),
            cspec(0), cspec(N_HEADS), cspec(2 * N_HEADS),
            row_spec, row_spec, row_spec,
            st_spec,
        ],
        out_specs=[
            pl.BlockSpec((None, tb, LANES), lambda b, h, tt: (b, tt, h)),
            st_spec,
        ],
        out_shape=[
            jax.ShapeDtypeStruct((bsz, t, BR_W), BF16),
            jax.ShapeDtypeStruct((bsz, N_HEADS, HEAD_DIM, HEAD_DIM), F32),
        ],
        scratch_shapes=[pltpu.VMEM((HEAD_DIM, HEAD_DIM), F32)]
        + [pltpu.VMEM((tb + SUBLANES, LANES), F32)] * 3,
        compiler_params=_cparams(("parallel", "parallel", "arbitrary")),
        name="gdn",
    )(proj, proj, proj, proj, proj, conv_w, conv_w, conv_w, conv0, conv0, conv0, alog, dtb, norm, s0)


def _log_sigmoid_pair(z):
    soft = jnp.log1p(jnp.exp(-jnp.abs(z)))
    ls = jnp.minimum(z, 0.0) - soft
    return ls, ls - z


def _sbp_body(q_ref, k_ref, v_ref, bias_ref, o_ref, acc_ref, run_ref, *, blk):
    qi = pl.program_id(2)
    q = q_ref[...].astype(BF16)
    bias = bias_ref[0:1, 0:1]
    scale = HEAD_DIM ** -0.5
    acc_ref[...] = jnp.zeros_like(acc_ref)
    run_ref[...] = jnp.zeros_like(run_ref)
    row = _iota2((blk, blk), 0)
    col = _iota2((blk, blk), 1)
    newer = (row > col).astype(BF16)

    def key_block(i, carry):
        kb = qi - i
        rows = pl.ds(pl.multiple_of(kb * blk, blk), blk)
        z = _dot_nt(q, k_ref[rows, :]) * scale + bias
        ls, lf = _log_sigmoid_pair(z)
        causal = (col + kb * blk) < (row + qi * blk)
        lf = jnp.where(causal, lf, 0.0)
        after = _dot_exact_rhs(lf, newer)
        w = jnp.where(causal, jnp.exp(ls + after + run_ref[...]), 0.0)
        acc_ref[...] += _dot(w, v_ref[rows, :])
        run_ref[...] += after[:, 0:1] + lf[:, 0:1]
        return carry

    lax.fori_loop(0, qi + 1, key_block, 0)
    o_ref[...] = acc_ref[...].astype(o_ref.dtype)


def _sb_prompt(proj, bias_rows):
    bsz, t, _ = proj.shape
    blk = min(SB_BLOCK, t)
    kv_spec = lambda base: pl.BlockSpec((None, t, LANES), lambda b, h, qi: (b, 0, base + h))
    return pl.pallas_call(
        functools.partial(_sbp_body, blk=blk),
        grid=(bsz, N_HEADS, t // blk),
        in_specs=[
            pl.BlockSpec((None, blk, LANES), lambda b, h, qi: (b, qi, BQ_BLK + h)),
            kv_spec(BK_BLK), kv_spec(BV_BLK),
            pl.BlockSpec((None, 1, LANES), lambda b, h, qi: (h, 0, 0)),
        ],
        out_specs=pl.BlockSpec((None, blk, LANES), lambda b, h, qi: (b, qi, h)),
        out_shape=jax.ShapeDtypeStruct((bsz, t, BR_W), BF16),
        scratch_shapes=[pltpu.VMEM((blk, HEAD_DIM), F32), pltpu.VMEM((blk, 1), F32)],
        compiler_params=_cparams(("parallel", "parallel", "arbitrary")),
        name="sb_prompt",
    )(proj, proj, proj, bias_rows)


def _sbs_body(pt_ref, q_ref, kn_ref, vn_ref, kp_ref, vp_ref, bias_ref, o_ref,
              qbd_ref, acc_ref, run_ref, knew_ref, vnew_ref, *, tpad, t_valid):
    j = pl.program_id(1)
    scale = HEAD_DIM ** -0.5
    row = _iota2((PAGE, LANES), 0)
    col = _iota2((PAGE, LANES), 1)
    newer = (_iota2((PAGE, PAGE), 1) > _iota2((PAGE, PAGE), 0)).astype(BF16)

    def page(k, v, mask):
        z = _dot_nt(k, qbd_ref[...]) * scale + bias_ref[...]
        ls, lf = _log_sigmoid_pair(z)
        if mask is not None:
            lf = jnp.where(mask, lf, 0.0)
        after = _dot_exact_lhs(newer, lf)
        w = jnp.exp(ls + after + run_ref[...])
        if mask is not None:
            w = jnp.where(mask, w, 0.0)
        acc_ref[...] += _dot_tn(w, v)
        run_ref[...] += after[0:1, :] + lf[0:1, :]

    @pl.when(j == 0)
    def _():
        q = q_ref[...]
        tiled = jnp.concatenate([q] * (LANES // tpad), axis=0)
        r = _iota2((LANES, BR_W), 0)
        c = _iota2((LANES, BR_W), 1)
        qbd_ref[...] = jnp.where(r // tpad == c // HEAD_DIM, tiled, 0.0).astype(BF16)
        acc_ref[...] = jnp.zeros_like(acc_ref)
        run_ref[...] = jnp.zeros_like(run_ref)
        knew_ref[...] = jnp.zeros_like(knew_ref)
        vnew_ref[...] = jnp.zeros_like(vnew_ref)
        knew_ref[0:tpad, :] = kn_ref[...].astype(BF16)
        vnew_ref[0:tpad, :] = vn_ref[...].astype(BF16)
        page(knew_ref[...], vnew_ref[...], (row < col % tpad) & (row < t_valid))

    @pl.when(j > 0)
    def _():
        page(kp_ref[...], vp_ref[...], None)

    @pl.when(j == pl.num_programs(1) - 1)
    def _():
        for hh in range(N_HEADS):
            cols = slice(hh * HEAD_DIM, (hh + 1) * HEAD_DIM)
            o_ref[:, cols] = acc_ref[hh * tpad:(hh + 1) * tpad, cols].astype(o_ref.dtype)


def _sb_sample(proj, cache_k, cache_v, page_table, bias_row, layer, *, t_valid):
    bsz, tpad, _ = proj.shape
    n_pages = page_table.shape[1]
    assert N_HEADS * tpad <= LANES and cache_k.shape[2] == PAGE

    def pspec(blk):
        return pl.BlockSpec((None, tpad, BR_W), lambda b, j, pt: (b, 0, blk // N_HEADS))

    def cache_spec():
        return pl.BlockSpec(
            (None, None, PAGE, BR_W),
            lambda b, j, pt: (layer, pt[b, jnp.where(j == 0, n_pages - 1, n_pages - j)], 0, 0))

    return pl.pallas_call(
        functools.partial(_sbs_body, tpad=tpad, t_valid=t_valid),
        grid_spec=pltpu.PrefetchScalarGridSpec(
            num_scalar_prefetch=1,
            grid=(bsz, n_pages + 1),
            in_specs=[
                pspec(BQ_BLK), pspec(BK_BLK), pspec(BV_BLK),
                cache_spec(), cache_spec(),
                pl.BlockSpec((1, LANES), lambda b, j, pt: (0, 0)),
            ],
            out_specs=pl.BlockSpec((None, tpad, BR_W), lambda b, j, pt: (b, 0, 0)),
            scratch_shapes=[
                pltpu.VMEM((LANES, BR_W), BF16),
                pltpu.VMEM((LANES, BR_W), F32),
                pltpu.VMEM((1, LANES), F32),
                pltpu.VMEM((PAGE, BR_W), BF16),
                pltpu.VMEM((PAGE, BR_W), BF16),
            ],
        ),
        out_shape=jax.ShapeDtypeStruct((bsz, tpad, BR_W), BF16),
        compiler_params=_cparams(("parallel", "arbitrary")),
        name="sb_sample",
    )(page_table, proj, proj, proj, cache_k, cache_v, bias_row)


def _regroup_w_in(w_in):
    a_end = 4 * BR_W
    b_end = a_end + 3 * BR_W
    c_end = b_end + 3 * BR_W
    z_end = c_end + BR_W
    ab_end = z_end + 2 * N_HEADS
    pad = PROJ_W - (w_in.shape[-1] - 2 * N_HEADS) - 2 * N_HEADS
    parts = [w_in[..., ab_end:], w_in[..., :z_end], w_in[..., z_end:ab_end],
             jnp.zeros(w_in.shape[:-1] + (pad,), w_in.dtype)]
    return jnp.concatenate(parts, axis=-1).astype(BF16)


def _pad_lanes(v):
    return jnp.pad(v.astype(F32), ((0, 0), (0, LANES - v.shape[-1])))


def _stream_layer(x, lw, states, attn_fn, *, tm, tf, tn, tb, t_valid):
    bsz, t, _ = x.shape
    x2 = x.reshape(bsz * t, D_MODEL)
    h = _ffn(x2, lw["f1_pre"], lw["f1_post"], lw["f1_wg"], lw["f1_wu"], lw["f1_wd"], tm=tm, tf=tf)
    proj = _inproj(h, lw["m_pre"], lw["w_in"], tm=tm, tn=tn).reshape(bsz, t, PROJ_W)
    oa, sa = _hgrn(proj, lw["lb"], lw["norm_a"], states["hgrn"], tb=tb, t_valid=t_valid)
    ob = attn_fn(proj)
    oc, sc = _gdn(proj, lw["conv_w"], states["conv"], lw["alog"], lw["dtb"], lw["norm_c"], states["gdn"],
                  tb=tb, t_valid=t_valid)
    m = bsz * t
    h = _merge(oa.reshape(m, BR_W), ob.reshape(m, BR_W), oc.reshape(m, BR_W), proj.reshape(m, PROJ_W),
               lw["w_cat"], lw["m_post"], h, tm=tm)
    y = _ffn(h, lw["f2_pre"], lw["f2_post"], lw["f2_wg"], lw["f2_wu"], lw["f2_wd"], tm=tm, tf=tf)
    return y.reshape(bsz, t, D_MODEL), proj, sa, sc


def kernel(x_prompt, x_sample, cache_k, cache_v, state_hgrn, state_gdn, state_conv, page_table,
           ffn1_norm_pre, ffn1_norm_post, ffn1_w_gate, ffn1_w_up, ffn1_w_down,
           mix_norm_pre, mix_norm_post, w_in, hgrn_lb_raw, hgrn_norm,
           gdn_conv_w, gdn_a_log, gdn_dt_bias, gdn_norm, sb_bias, w_branch, w_out,
           ffn2_norm_pre, ffn2_norm_post, ffn2_w_gate, ffn2_w_up, ffn2_w_down):
    depth = w_in.shape[0]
    bp, seq, _ = x_prompt.shape
    db, dec_seq, _ = x_sample.shape
    n_phys = cache_k.shape[1]
    tpad = SUBLANES * pl.cdiv(dec_seq, SUBLANES)

    lbs = jnp.cumsum(jax.nn.softmax(hgrn_lb_raw.astype(F32), axis=0), axis=0)
    lbs = lbs - lbs[:1]

    w_in_r = _regroup_w_in(w_in)
    w_cat = jnp.concatenate([w_branch.reshape(depth, 3 * BR_W, D_MODEL), w_out], axis=1).astype(BF16)
    bf = lambda w: w.astype(BF16)
    f1 = (bf(ffn1_w_gate), bf(ffn1_w_up), bf(ffn1_w_down))
    f2 = (bf(ffn2_w_gate), bf(ffn2_w_up), bf(ffn2_w_down))
    cache_k2 = cache_k.reshape(depth, n_phys, PAGE, BR_W)
    cache_v2 = cache_v.reshape(depth, n_phys, PAGE, BR_W)
    alog_p, dtb_p = _pad_lanes(gdn_a_log), _pad_lanes(gdn_dt_bias)
    bias_prompt = jnp.broadcast_to(sb_bias.astype(F32)[:, :, None, None], (depth, N_HEADS, 1, LANES))
    bias_sample = _pad_lanes(jnp.repeat(sb_bias.astype(F32), tpad, axis=1))

    row = lambda v: v.reshape(1, -1)
    zero_states = dict(hgrn=jnp.zeros((bp, N_HEADS, HEAD_DIM, HEAD_DIM), F32),
                       gdn=jnp.zeros((bp, N_HEADS, HEAD_DIM, HEAD_DIM), F32),
                       conv=jnp.zeros((bp, CONV_K - 1, 3 * BR_W), F32))

    yp = x_prompt
    ys = jnp.pad(x_sample, ((0, 0), (0, tpad - dec_seq), (0, 0)))
    outs = [[] for _ in range(10)]
    for l in range(depth):
        lw = dict(
            f1_pre=row(ffn1_norm_pre[l]), f1_post=row(ffn1_norm_post[l]), f1_wg=f1[0][l], f1_wu=f1[1][l], f1_wd=f1[2][l],
            m_pre=row(mix_norm_pre[l]), m_post=row(mix_norm_post[l]), w_in=w_in_r[l], lb=row(lbs[l]),
            norm_a=row(hgrn_norm[l]), conv_w=gdn_conv_w[l], alog=alog_p[l:l + 1], dtb=dtb_p[l:l + 1],
            norm_c=row(gdn_norm[l]), w_cat=w_cat[l],
            f2_pre=row(ffn2_norm_pre[l]), f2_post=row(ffn2_norm_post[l]), f2_wg=f2[0][l], f2_wu=f2[1][l], f2_wd=f2[2][l],
        )
        yp, pp, sa, sc = _stream_layer(
            yp, lw, zero_states, lambda p: _sb_prompt(p, bias_prompt[l]),
            tm=512, tf=512, tn=512, tb=256, t_valid=256)
        sample_states = dict(hgrn=state_hgrn[l], gdn=state_gdn[l], conv=state_conv[l])
        ys, ps, sas, scs = _stream_layer(
            ys, lw, sample_states,
            lambda p: _sb_sample(p, cache_k2, cache_v2, page_table, bias_sample[l:l + 1], l, t_valid=dec_seq),
            tm=db * tpad, tf=512, tn=512, tb=tpad, t_valid=dec_seq)

        def cols(p, blk, width):
            return p[:, :, blk * LANES:blk * LANES + width]

        keep = CONV_K - 1
        vals = (sa, sas, sc, scs,
                cols(pp, CQ_BLK, 3 * BR_W)[:, seq - keep:seq],
                cols(ps, CQ_BLK, 3 * BR_W)[:, dec_seq - keep:dec_seq],
                cols(pp, BK_BLK, BR_W).reshape(bp, seq, N_HEADS, HEAD_DIM),
                cols(pp, BV_BLK, BR_W).reshape(bp, seq, N_HEADS, HEAD_DIM),
                cols(ps, BK_BLK, BR_W)[:, :dec_seq].reshape(db, dec_seq, N_HEADS, HEAD_DIM),
                cols(ps, BV_BLK, BR_W)[:, :dec_seq].reshape(db, dec_seq, N_HEADS, HEAD_DIM))
        for acc, v in zip(outs, vals):
            acc.append(v)

    stacked = [jnp.stack(v) for v in outs]
    return (yp, ys[:, :dec_seq], *stacked)
```

```python
import functools

import jax
import jax.numpy as jnp
from jax import lax
from jax.experimental import pallas as pl
from jax.experimental.pallas import tpu as pltpu

F32 = jnp.float32
BF16 = jnp.bfloat16

D_MODEL = 2048
D_FF = 5632
N_HEADS = 8
HEAD_DIM = 128
BR_W = N_HEADS * HEAD_DIM
CONV_K = 4
RMS_EPS = 1e-6
LANES = 128
SUBLANES = 8
VMEM_LIMIT = 56 * 1024 * 1024

GATE_BLK = 0
AQ_BLK, AF_BLK, AI_BLK, AG_BLK = 48, 56, 64, 72
BQ_BLK, BK_BLK, BV_BLK = 80, 88, 96
CQ_BLK, CK_BLK, CV_BLK = 104, 112, 120
CZ_BLK = 128
AB_BLK = 136
PROJ_W = 140 * LANES

HGRN_CHUNK = 32
GDN_CHUNK = 64
SB_BLOCK = 256
SB_HEADS_PER_STEP = 2
PAGE = 128
PAGES_PER_STEP = 4


def _cparams(semantics):
    return pltpu.CompilerParams(dimension_semantics=semantics, vmem_limit_bytes=VMEM_LIMIT)


def _rms(x, g):
    ms = jnp.mean(x * x, axis=-1, keepdims=True)
    return x * lax.rsqrt(ms + RMS_EPS) * g


def _silu(x):
    return x * jax.nn.sigmoid(x)


def _dot(a, b):
    return jnp.dot(a.astype(BF16), b.astype(BF16), preferred_element_type=F32)


def _dot_nt(a, b):
    return lax.dot_general(a.astype(BF16), b.astype(BF16), (((1,), (1,)), ((), ())),
                           preferred_element_type=F32)


def _dot_tn(a, b):
    return lax.dot_general(a.astype(BF16), b.astype(BF16), (((0,), (0,)), ((), ())),
                           preferred_element_type=F32)


def _split2(x):
    hi = x.astype(BF16)
    return hi, (x - hi.astype(F32)).astype(BF16)


def _sum_lhs(m_bf16, x):
    hi, lo = _split2(x)
    d = functools.partial(jnp.dot, preferred_element_type=F32)
    return d(m_bf16, hi) + d(m_bf16, lo)


def _sum_rhs(x, m_bf16):
    hi, lo = _split2(x)
    d = functools.partial(jnp.dot, preferred_element_type=F32)
    return d(hi, m_bf16) + d(lo, m_bf16)


def _iota2(shape, dim):
    return lax.broadcasted_iota(jnp.int32, shape, dim)


def _head(h):
    return slice(h * HEAD_DIM, (h + 1) * HEAD_DIM)


def _ffn_body(x_ref, pre_ref, post_ref, wg_ref, wu_ref, wd_ref, o_ref, xn_ref, acc_ref):
    j = pl.program_id(1)

    @pl.when(j == 0)
    def _():
        xn_ref[...] = _rms(x_ref[...], pre_ref[...]).astype(BF16)
        acc_ref[...] = jnp.zeros_like(acc_ref)

    xn = xn_ref[...]
    g = jnp.dot(xn, wg_ref[...], preferred_element_type=F32)
    u = jnp.dot(xn, wu_ref[...], preferred_element_type=F32)
    acc_ref[...] += _dot(_silu(g) * u, wd_ref[...])

    @pl.when(j == pl.num_programs(1) - 1)
    def _():
        o_ref[...] = x_ref[...] + 0.5 * _rms(acc_ref[...], post_ref[...])


def _ffn(x, pre, post, wg, wu, wd, *, tm, tf):
    m = x.shape[0]
    return pl.pallas_call(
        _ffn_body,
        grid=(m // tm, D_FF // tf),
        in_specs=[
            pl.BlockSpec((tm, D_MODEL), lambda i, j: (i, 0)),
            pl.BlockSpec((1, D_MODEL), lambda i, j: (0, 0)),
            pl.BlockSpec((1, D_MODEL), lambda i, j: (0, 0)),
            pl.BlockSpec((D_MODEL, tf), lambda i, j: (0, j)),
            pl.BlockSpec((D_MODEL, tf), lambda i, j: (0, j)),
            pl.BlockSpec((tf, D_MODEL), lambda i, j: (j, 0)),
        ],
        out_specs=pl.BlockSpec((tm, D_MODEL), lambda i, j: (i, 0)),
        out_shape=jax.ShapeDtypeStruct((m, D_MODEL), F32),
        scratch_shapes=[pltpu.VMEM((tm, D_MODEL), BF16), pltpu.VMEM((tm, D_MODEL), F32)],
        compiler_params=_cparams(("parallel", "arbitrary")),
        name="ffn",
    )(x, pre, post, wg, wu, wd)


def _inproj_body(x_ref, pre_ref, w_ref, o_ref, xn_ref):
    @pl.when(pl.program_id(1) == 0)
    def _():
        xn_ref[...] = _rms(x_ref[...], pre_ref[...]).astype(BF16)

    o_ref[...] = jnp.dot(xn_ref[...], w_ref[...], preferred_element_type=F32)


def _inproj(x, pre, w, *, tm, tn):
    m = x.shape[0]
    return pl.pallas_call(
        _inproj_body,
        grid=(m // tm, PROJ_W // tn),
        in_specs=[
            pl.BlockSpec((tm, D_MODEL), lambda i, j: (i, 0)),
            pl.BlockSpec((1, D_MODEL), lambda i, j: (0, 0)),
            pl.BlockSpec((D_MODEL, tn), lambda i, j: (0, j)),
        ],
        out_specs=pl.BlockSpec((tm, tn), lambda i, j: (i, j)),
        out_shape=jax.ShapeDtypeStruct((m, PROJ_W), F32),
        scratch_shapes=[pltpu.VMEM((tm, D_MODEL), BF16)],
        compiler_params=_cparams(("parallel", "arbitrary")),
        name="inproj",
    )(x, pre, w)


def _merge_body(oa_ref, ob_ref, oc_ref, gate_ref, w_ref, post_ref, h_ref, out_ref, acc_ref, y_ref):
    j = pl.program_id(1)

    def branch(o_ref, first):
        m = jnp.dot(o_ref[...], w_ref[...], preferred_element_type=F32) * jax.nn.sigmoid(gate_ref[...])
        if first:
            acc_ref[...] = m
        else:
            acc_ref[...] += m

    pl.when(j == 0)(lambda: branch(oa_ref, True))
    pl.when(j == 1)(lambda: branch(ob_ref, False))
    pl.when(j == 2)(lambda: branch(oc_ref, False))

    @pl.when(j == 3)
    def _():
        y_ref[...] = _dot(acc_ref[:, :BR_W], w_ref[...])

    @pl.when(j == 4)
    def _():
        y = y_ref[...] + _dot(acc_ref[:, BR_W:], w_ref[...])
        out_ref[...] = h_ref[...] + _rms(y, post_ref[...])


def _merge(oa, ob, oc, proj, w_cat, post, h, *, tm):
    m = h.shape[0]
    o_spec = pl.BlockSpec((tm, BR_W), lambda i, j: (i, 0))
    return pl.pallas_call(
        _merge_body,
        grid=(m // tm, 5),
        in_specs=[
            o_spec, o_spec, o_spec,
            pl.BlockSpec((tm, D_MODEL), lambda i, j: (i, jnp.minimum(j, 2))),
            pl.BlockSpec((BR_W, D_MODEL), lambda i, j: (j, 0)),
            pl.BlockSpec((1, D_MODEL), lambda i, j: (0, 0)),
            pl.BlockSpec((tm, D_MODEL), lambda i, j: (i, 0)),
        ],
        out_specs=pl.BlockSpec((tm, D_MODEL), lambda i, j: (i, 0)),
        out_shape=jax.ShapeDtypeStruct((m, D_MODEL), F32),
        scratch_shapes=[pltpu.VMEM((tm, D_MODEL), F32), pltpu.VMEM((tm, D_MODEL), F32)],
        compiler_params=_cparams(("parallel", "arbitrary")),
        name="merge",
    )(oa, ob, oc, proj, w_cat, post, h)


def _hgrn_body(q_ref, f_ref, i_ref, g_ref, lb_ref, norm_ref, s0_ref, o_ref, sout_ref, st_ref,
               *, tb, chunk, t_valid):
    tt = pl.program_id(1)

    @pl.when(tt == 0)
    def _():
        for h in range(N_HEADS):
            st_ref[h] = s0_ref[h].T

    lb = lb_ref[...]
    one_m_lb = 1.0 - lb
    norm = norm_ref[...]
    incl = _iota2((chunk, chunk), 0) >= _iota2((chunk, chunk), 1)
    tril = incl.astype(BF16)
    scale = HEAD_DIM ** -0.5
    states = [st_ref[h] for h in range(N_HEADS)]

    for ci in range(tb // chunk):
        rows = slice(ci * chunk, (ci + 1) * chunk)
        sig = jax.nn.sigmoid(f_ref[rows, :])
        log_f = jnp.log(lb + one_m_lb * sig)
        k = one_m_lb * (1.0 - sig)
        if t_valid < tb:
            ok = (_iota2((chunk, 1), 0) + ci * chunk) < t_valid
            log_f = jnp.where(ok, log_f, 0.0)
            k = jnp.where(ok, k, 0.0)
        b = _sum_lhs(tril, log_f)
        qh = _silu(q_ref[rows, :]) * scale * jnp.exp(b)
        kh = k * jnp.exp(-b)
        b_last = b[chunk - 1:chunk, :]
        kd = k * jnp.exp(b_last - b)
        dec = jnp.exp(b_last)
        v = i_ref[rows, :]
        gate = _silu(g_ref[rows, :])
        for h in range(N_HEADS):
            cs = _head(h)
            st = states[h]
            attn = jnp.where(incl, _dot_nt(qh[:, cs], kh[:, cs]), 0.0)
            o = _dot_nt(qh[:, cs], st) + _dot(attn, v[:, cs])
            states[h] = st * dec[:, cs] + _dot_tn(v[:, cs], kd[:, cs])
            o_ref[rows, cs] = (_rms(o, norm) * gate[:, cs]).astype(o_ref.dtype)

    for h in range(N_HEADS):
        st_ref[h] = states[h]

    @pl.when(tt == pl.num_programs(1) - 1)
    def _():
        for h in range(N_HEADS):
            sout_ref[h] = states[h].T


def _hgrn(proj, lb, norm, s0, *, tb, t_valid):
    bsz, t, _ = proj.shape
    chunk = min(HGRN_CHUNK, tb)

    def pspec(blk):
        return pl.BlockSpec((None, tb, BR_W), lambda b, tt: (b, tt, blk // N_HEADS))

    st_spec = pl.BlockSpec((None, N_HEADS, HEAD_DIM, HEAD_DIM), lambda b, tt: (b, 0, 0, 0))
    return pl.pallas_call(
        functools.partial(_hgrn_body, tb=tb, chunk=chunk, t_valid=t_valid),
        grid=(bsz, t // tb),
        in_specs=[
            pspec(AQ_BLK), pspec(AF_BLK), pspec(AI_BLK), pspec(AG_BLK),
            pl.BlockSpec((1, BR_W), lambda b, tt: (0, 0)),
            pl.BlockSpec((1, LANES), lambda b, tt: (0, 0)),
            st_spec,
        ],
        out_specs=[
            pl.BlockSpec((None, tb, BR_W), lambda b, tt: (b, tt, 0)),
            st_spec,
        ],
        out_shape=[
            jax.ShapeDtypeStruct((bsz, t, BR_W), BF16),
            jax.ShapeDtypeStruct((bsz, N_HEADS, HEAD_DIM, HEAD_DIM), F32),
        ],
        scratch_shapes=[pltpu.VMEM((N_HEADS, HEAD_DIM, HEAD_DIM), F32)],
        compiler_params=_cparams(("parallel", "arbitrary")),
        name="hgrn",
    )(proj, proj, proj, proj, lb, norm, s0)


def _gdn_body(q_ref, k_ref, v_ref, z_ref, ab_ref, w_ref, c0_ref, alog_ref, dt_ref, norm_ref, s0_ref,
              o_ref, sout_ref, s_ref, xq_ref, xk_ref, xv_ref, *, tb, chunk, t_valid):
    tt = pl.program_id(1)
    pad = SUBLANES
    keep = CONV_K - 1
    xrefs = (xq_ref, xk_ref, xv_ref)

    @pl.when(tt == 0)
    def _():
        s_ref[...] = s0_ref[...]
        for n, x_ref in enumerate(xrefs):
            x_ref[pad - keep:pad, :] = c0_ref[:, n * BR_W:(n + 1) * BR_W]

    for n, (x_ref, src_ref) in enumerate(zip(xrefs, (q_ref, k_ref, v_ref))):
        w = w_ref[:, n * BR_W:(n + 1) * BR_W]
        x_ref[pad:pad + tb, :] = src_ref[...]
        y = x_ref[pad:pad + tb, :] * w[keep:keep + 1, :]
        for j in range(keep):
            y = y + x_ref[pad - keep + j:pad - keep + j + tb, :] * w[j:j + 1, :]
        x_ref[pad - keep:pad, :] = x_ref[pad + tb - keep:pad + tb, :]
        x_ref[pad:pad + tb, :] = _silu(y)

    neg_a = -jnp.exp(alog_ref[...])
    norm = norm_ref[...]
    row = _iota2((chunk, chunk), 0)
    col = _iota2((chunk, chunk), 1)
    incl = row >= col
    strict = row > col
    tril = incl.astype(BF16)
    eye = (row == col).astype(F32)
    states = [s_ref[h] for h in range(N_HEADS)]
    n_chunks = tb // chunk
    chains = [(ci, h) for ci in range(n_chunks) for h in range(N_HEADS)]

    zg, c = [], {}
    for ci in range(n_chunks):
        rows = slice(ci * chunk, (ci + 1) * chunk)
        xrows = slice(pad + ci * chunk, pad + (ci + 1) * chunk)
        ab = ab_ref[rows, :]
        log_a_all = neg_a * jax.nn.softplus(ab + dt_ref[...])
        beta_all = jax.nn.sigmoid(ab)
        if t_valid < tb:
            ok = (_iota2((chunk, 1), 0) + ci * chunk) < t_valid
            log_a_all = jnp.where(ok, log_a_all, 0.0)
            beta_all = jnp.where(ok, beta_all, 0.0)
        g_all = _sum_lhs(tril, log_a_all)
        g_rows = g_all.T
        eg_all = jnp.exp(g_all)
        zg.append(_silu(z_ref[rows, :]))
        for h in range(N_HEADS):
            cs = _head(h)
            qc, kc, vc = xq_ref[xrows, cs], xk_ref[xrows, cs], xv_ref[xrows, cs]
            qn = qc * lax.rsqrt(jnp.sum(qc * qc, axis=-1, keepdims=True) + 1e-6) * (HEAD_DIM ** -0.5)
            kn = kc * lax.rsqrt(jnp.sum(kc * kc, axis=-1, keepdims=True) + 1e-6)
            beta = beta_all[:, N_HEADS + h:N_HEADS + h + 1]
            g = g_all[:, h:h + 1]
            eg = eg_all[:, h:h + 1]
            g_last = g[chunk - 1:chunk, :]
            c[ci, h] = dict(
                qn=qn.astype(BF16), kn=kn.astype(BF16), beta=beta,
                gam=jnp.where(incl, jnp.exp(g - g_rows[h:h + 1, :]), 0.0),
                rhs=jnp.concatenate([vc * beta, kn * (beta * eg)], axis=1).astype(BF16),
                q_eg=(qn * eg).astype(BF16), k_dec=(kn * jnp.exp(g_last - g)).astype(BF16),
                s_dec=jnp.exp(g_last))

    kk = {ch: _dot_nt(c[ch]["kn"], c[ch]["kn"]) for ch in chains}
    qk = {ch: _dot_nt(c[ch]["qn"], c[ch]["kn"]) for ch in chains}
    a = {ch: jnp.where(strict, kk[ch] * c[ch]["gam"] * c[ch]["beta"], 0.0) for ch in chains}
    aqk = {ch: (qk[ch] * c[ch]["gam"]).astype(BF16) for ch in chains}
    t_inv = {ch: eye - jnp.where(row // 2 == col // 2, a[ch], 0.0) for ch in chains}
    size = 2
    while size < chunk:
        below = (row // (2 * size) == col // (2 * size)) & (row // size > col // size)
        at = {ch: _dot(jnp.where(below, a[ch], 0.0), t_inv[ch]) for ch in chains}
        t_inv = {ch: t_inv[ch] - _dot(t_inv[ch], at[ch]) for ch in chains}
        size *= 2
    uw = {ch: _dot(t_inv[ch], c[ch]["rhs"]) for ch in chains}

    for ci in range(n_chunks):
        rows = slice(ci * chunk, (ci + 1) * chunk)
        heads = range(N_HEADS)
        ws = [_dot(jnp.concatenate([uw[ci, h][:, HEAD_DIM:].astype(BF16), c[ci, h]["q_eg"]], axis=0), states[h])
              for h in heads]
        v_new = [(uw[ci, h][:, :HEAD_DIM] - ws[h][:chunk]).astype(BF16) for h in heads]
        o = [ws[h][chunk:] + _dot(aqk[ci, h], v_new[h]) for h in heads]
        states = [states[h] * c[ci, h]["s_dec"] + _dot_tn(c[ci, h]["k_dec"], v_new[h]) for h in heads]
        for h in heads:
            o_ref[rows, _head(h)] = (_rms(o[h], norm) * zg[ci][:, _head(h)]).astype(o_ref.dtype)

    for h in range(N_HEADS):
        s_ref[h] = states[h]

    @pl.when(tt == pl.num_programs(1) - 1)
    def _():
        for h in range(N_HEADS):
            sout_ref[h] = states[h]


def _gdn(proj, conv_w, conv0, alog, dtb, norm, s0, *, tb, t_valid):
    bsz, t, _ = proj.shape
    chunk = min(GDN_CHUNK, tb)
    keep = CONV_K - 1

    def pspec(blk):
        return pl.BlockSpec((None, tb, BR_W), lambda b, tt: (b, tt, blk // N_HEADS))

    row_spec = pl.BlockSpec((1, LANES), lambda b, tt: (0, 0))
    st_spec = pl.BlockSpec((None, N_HEADS, HEAD_DIM, HEAD_DIM), lambda b, tt: (b, 0, 0, 0))
    return pl.pallas_call(
        functools.partial(_gdn_body, tb=tb, chunk=chunk, t_valid=t_valid),
        grid=(bsz, t // tb),
        in_specs=[
            pspec(CQ_BLK), pspec(CK_BLK), pspec(CV_BLK), pspec(CZ_BLK),
            pl.BlockSpec((None, tb, LANES), lambda b, tt: (b, tt, AB_BLK)),
            pl.BlockSpec((CONV_K, 3 * BR_W), lambda b, tt: (0, 0)),
            pl.BlockSpec((None, keep, 3 * BR_W), lambda b, tt: (b, 0, 0)),
            row_spec, row_spec, row_spec,
            st_spec,
        ],
        out_specs=[
            pl.BlockSpec((None, tb, BR_W), lambda b, tt: (b, tt, 0)),
            st_spec,
        ],
        out_shape=[
            jax.ShapeDtypeStruct((bsz, t, BR_W), BF16),
            jax.ShapeDtypeStruct((bsz, N_HEADS, HEAD_DIM, HEAD_DIM), F32),
        ],
        scratch_shapes=[pltpu.VMEM((N_HEADS, HEAD_DIM, HEAD_DIM), F32)]
        + [pltpu.VMEM((tb + SUBLANES, BR_W), F32)] * 3,
        compiler_params=_cparams(("parallel", "arbitrary")),
        name="gdn",
    )(proj, proj, proj, proj, proj, conv_w, conv0, alog, dtb, norm, s0)


def _log_sigmoid_pair(z):
    soft = jnp.log(1.0 + jnp.exp(-jnp.abs(z)))
    ls = jnp.minimum(z, 0.0) - soft
    return ls, ls - z


def _sbp_body(q_ref, k_ref, v_ref, bias_ref, o_ref, acc_ref, run_ref, *, blk, n_hd):
    qi = pl.program_id(2)
    heads = range(n_hd)
    q = [q_ref[:, _head(h)].astype(BF16) for h in heads]
    bias = [bias_ref[h, 0:1, 0:1] for h in heads]
    scale = HEAD_DIM ** -0.5
    row = _iota2((blk, blk), 0)
    col = _iota2((blk, blk), 1)
    newer = (row > col).astype(BF16)
    earlier = col < row

    def key_block(kb, diagonal):
        rows = pl.ds(pl.multiple_of(kb * blk, blk), blk)
        zs = [_dot_nt(q[h], k_ref[rows, _head(h)]) * scale + bias[h] for h in heads]
        pairs = [_log_sigmoid_pair(z) for z in zs]
        lfs = [jnp.where(earlier, lf, 0.0) if diagonal else lf for _, lf in pairs]
        afters = [_sum_rhs(lf, newer) for lf in lfs]
        for h in heads:
            w = jnp.exp(pairs[h][0] + afters[h] + run_ref[h])
            if diagonal:
                w = jnp.where(earlier, w, 0.0)
            acc_ref[:, _head(h)] += _dot(w, v_ref[rows, _head(h)])
            run_ref[h] += afters[h][:, 0:1] + lfs[h][:, 0:1]

    acc_ref[...] = jnp.zeros_like(acc_ref)
    run_ref[...] = jnp.zeros_like(run_ref)
    key_block(qi, True)

    def older(i, carry):
        key_block(qi - i, False)
        return carry

    lax.fori_loop(1, qi + 1, older, 0)
    o_ref[...] = acc_ref[...].astype(o_ref.dtype)


def _sb_prompt(proj, bias_rows):
    bsz, t, _ = proj.shape
    blk = min(SB_BLOCK, t)
    n_hd = SB_HEADS_PER_STEP
    wide = n_hd * HEAD_DIM
    kv_spec = lambda base: pl.BlockSpec((None, t, wide), lambda b, hp, qi: (b, 0, base // n_hd + hp))
    return pl.pallas_call(
        functools.partial(_sbp_body, blk=blk, n_hd=n_hd),
        grid=(bsz, N_HEADS // n_hd, t // blk),
        in_specs=[
            pl.BlockSpec((None, blk, wide), lambda b, hp, qi: (b, qi, BQ_BLK // n_hd + hp)),
            kv_spec(BK_BLK), kv_spec(BV_BLK),
            pl.BlockSpec((n_hd, 1, LANES), lambda b, hp, qi: (hp, 0, 0)),
        ],
        out_specs=pl.BlockSpec((None, blk, wide), lambda b, hp, qi: (b, qi, hp)),
        out_shape=jax.ShapeDtypeStruct((bsz, t, BR_W), BF16),
        scratch_shapes=[pltpu.VMEM((blk, wide), F32), pltpu.VMEM((n_hd, blk, 1), F32)],
        compiler_params=_cparams(("parallel", "parallel", "arbitrary")),
        name="sb_prompt",
    )(proj, proj, proj, bias_rows)


def _sbs_body(pt_ref, q_ref, kn_ref, vn_ref, *rest, tpad, t_valid, n_pg):
    kp_refs, vp_refs = rest[:n_pg], rest[n_pg:2 * n_pg]
    bias_ref, o_ref, qbd_ref, acc_ref, run_ref, knew_ref, vnew_ref = rest[2 * n_pg:]
    j = pl.program_id(1)
    scale = HEAD_DIM ** -0.5
    newer = (_iota2((PAGE, PAGE), 1) > _iota2((PAGE, PAGE), 0)).astype(BF16)

    def pages(ks, vs, run, mask):
        zs = [_dot_nt(k, qbd_ref[...]) * scale + bias_ref[...] for k in ks]
        pairs = [_log_sigmoid_pair(z) for z in zs]
        lfs = [lf if mask is None else jnp.where(mask, lf, 0.0) for _, lf in pairs]
        afters = [_sum_lhs(newer, lf) for lf in lfs]
        total = None
        for (ls, _), lf, after, v in zip(pairs, lfs, afters, vs):
            w = jnp.exp(ls + after + run)
            if mask is not None:
                w = jnp.where(mask, w, 0.0)
            part = _dot_tn(w, v)
            total = part if total is None else total + part
            run = run + after[0:1, :] + lf[0:1, :]
        return total, run

    def heads_on_lanes(ref):
        return jnp.concatenate([ref[pl.ds(h, PAGE, stride=N_HEADS), :] for h in range(N_HEADS)],
                               axis=1).astype(BF16)

    @pl.when(j == 0)
    def _():
        q = q_ref[...]
        tiled = jnp.concatenate([q] * (LANES // tpad), axis=0)
        r = _iota2((LANES, BR_W), 0)
        c = _iota2((LANES, BR_W), 1)
        qbd_ref[...] = jnp.where(r // tpad == c // HEAD_DIM, tiled, 0.0).astype(BF16)
        knew_ref[...] = jnp.zeros_like(knew_ref)
        vnew_ref[...] = jnp.zeros_like(vnew_ref)
        knew_ref[0:tpad, :] = kn_ref[...].astype(BF16)
        vnew_ref[0:tpad, :] = vn_ref[...].astype(BF16)
        row = _iota2((PAGE, LANES), 0)
        col = _iota2((PAGE, LANES), 1)
        mask = (row < col % tpad) & (row < t_valid)
        acc, run = pages([knew_ref[...]], [vnew_ref[...]], jnp.zeros((1, LANES), F32), mask)
        acc_ref[...] = acc
        run_ref[...] = run

    @pl.when(j > 0)
    def _():
        total, run = pages([heads_on_lanes(r) for r in kp_refs], [heads_on_lanes(r) for r in vp_refs],
                           run_ref[...], None)
        acc_ref[...] += total
        run_ref[...] = run

    @pl.when(j == pl.num_programs(1) - 1)
    def _():
        for hh in range(N_HEADS):
            o_ref[:, _head(hh)] = acc_ref[hh * tpad:(hh + 1) * tpad, _head(hh)].astype(o_ref.dtype)


def _sb_sample(proj, cache_k, cache_v, page_table, bias_row, layer, *, t_valid):
    bsz, tpad, _ = proj.shape
    n_pages = page_table.shape[1]
    n_pg = PAGES_PER_STEP
    assert N_HEADS * tpad <= LANES and cache_k.shape[2:] == (PAGE * N_HEADS, HEAD_DIM) and n_pages % n_pg == 0

    def pspec(blk):
        return pl.BlockSpec((None, tpad, BR_W), lambda b, j, pt: (b, 0, blk // N_HEADS))

    def cache_spec(g):
        def index(b, j, pt):
            return (layer, pt[b, n_pages - 1 - (jnp.maximum(j, 1) - 1) * n_pg - g], 0, 0)
        return pl.BlockSpec((None, None, PAGE * N_HEADS, HEAD_DIM), index)

    return pl.pallas_call(
        functools.partial(_sbs_body, tpad=tpad, t_valid=t_valid, n_pg=n_pg),
        grid_spec=pltpu.PrefetchScalarGridSpec(
            num_scalar_prefetch=1,
            grid=(bsz, n_pages // n_pg + 1),
            in_specs=[pspec(BQ_BLK), pspec(BK_BLK), pspec(BV_BLK)]
            + [cache_spec(g) for g in range(n_pg)] * 2
            + [pl.BlockSpec((1, LANES), lambda b, j, pt: (0, 0))],
            out_specs=pl.BlockSpec((None, tpad, BR_W), lambda b, j, pt: (b, 0, 0)),
            scratch_shapes=[
                pltpu.VMEM((LANES, BR_W), BF16),
                pltpu.VMEM((LANES, BR_W), F32),
                pltpu.VMEM((1, LANES), F32),
                pltpu.VMEM((PAGE, BR_W), BF16),
                pltpu.VMEM((PAGE, BR_W), BF16),
            ],
        ),
        out_shape=jax.ShapeDtypeStruct((bsz, tpad, BR_W), BF16),
        compiler_params=_cparams(("parallel", "arbitrary")),
        name="sb_sample",
    )(page_table, proj, proj, proj, *([cache_k] * n_pg), *([cache_v] * n_pg), bias_row)


def _regroup_w_in(w_in):
    a_end = 4 * BR_W
    b_end = a_end + 3 * BR_W
    c_end = b_end + 3 * BR_W
    z_end = c_end + BR_W
    ab_end = z_end + 2 * N_HEADS
    pad = PROJ_W - w_in.shape[-1]
    parts = [w_in[..., ab_end:], w_in[..., :z_end], w_in[..., z_end:ab_end],
             jnp.zeros(w_in.shape[:-1] + (pad,), w_in.dtype)]
    return jnp.concatenate(parts, axis=-1).astype(BF16)


def _pad_lanes(v):
    return jnp.pad(v.astype(F32), ((0, 0), (0, LANES - v.shape[-1])))


def _stream_layer(x, lw, states, attn_fn, *, tm, tf, tn, tb, t_valid):
    bsz, t, _ = x.shape
    m = bsz * t
    x2 = x.reshape(m, D_MODEL)
    h = _ffn(x2, lw["f1_pre"], lw["f1_post"], lw["f1_wg"], lw["f1_wu"], lw["f1_wd"], tm=min(tm, 512), tf=tf)
    proj = _inproj(h, lw["m_pre"], lw["w_in"], tm=tm, tn=tn).reshape(bsz, t, PROJ_W)
    oa, sa = _hgrn(proj, lw["lb"], lw["norm_a"], states["hgrn"], tb=tb, t_valid=t_valid)
    ob = attn_fn(proj)
    oc, sc = _gdn(proj, lw["conv_w"], states["conv"], lw["alog"], lw["dtb"], lw["norm_c"], states["gdn"],
                  tb=min(tb, 2 * GDN_CHUNK), t_valid=t_valid)
    h = _merge(oa.reshape(m, BR_W), ob.reshape(m, BR_W), oc.reshape(m, BR_W), proj.reshape(m, PROJ_W),
               lw["w_cat"], lw["m_post"], h, tm=min(tm, 512))
    y = _ffn(h, lw["f2_pre"], lw["f2_post"], lw["f2_wg"], lw["f2_wu"], lw["f2_wd"], tm=min(tm, 512), tf=tf)
    return y.reshape(bsz, t, D_MODEL), proj, sa, sc


def kernel(x_prompt, x_sample, cache_k, cache_v, state_hgrn, state_gdn, state_conv, page_table,
           ffn1_norm_pre, ffn1_norm_post, ffn1_w_gate, ffn1_w_up, ffn1_w_down,
           mix_norm_pre, mix_norm_post, w_in, hgrn_lb_raw, hgrn_norm,
           gdn_conv_w, gdn_a_log, gdn_dt_bias, gdn_norm, sb_bias, w_branch, w_out,
           ffn2_norm_pre, ffn2_norm_post, ffn2_w_gate, ffn2_w_up, ffn2_w_down):
    depth = w_in.shape[0]
    bp, seq, _ = x_prompt.shape
    db, dec_seq, _ = x_sample.shape
    tpad = SUBLANES * pl.cdiv(dec_seq, SUBLANES)

    lbs = jnp.cumsum(jax.nn.softmax(hgrn_lb_raw.astype(F32), axis=0), axis=0)
    lbs = lbs - lbs[:1]

    w_in_r = _regroup_w_in(w_in)
    w_cat = jnp.concatenate([w_branch.reshape(depth, 3 * BR_W, D_MODEL), w_out], axis=1).astype(BF16)
    bf = lambda w: w.astype(BF16)
    f1 = (bf(ffn1_w_gate), bf(ffn1_w_up), bf(ffn1_w_down))
    f2 = (bf(ffn2_w_gate), bf(ffn2_w_up), bf(ffn2_w_down))
    cache_k2 = cache_k.reshape(depth, cache_k.shape[1], PAGE * N_HEADS, HEAD_DIM)
    cache_v2 = cache_v.reshape(depth, cache_v.shape[1], PAGE * N_HEADS, HEAD_DIM)
    alog_p, dtb_p = _pad_lanes(gdn_a_log), _pad_lanes(gdn_dt_bias)
    bias_prompt = jnp.broadcast_to(sb_bias.astype(F32)[:, :, None, None], (depth, N_HEADS, 1, LANES))
    bias_sample = _pad_lanes(jnp.repeat(sb_bias.astype(F32), tpad, axis=1))

    row = lambda v: v.reshape(1, -1)
    zero_states = dict(hgrn=jnp.zeros((bp, N_HEADS, HEAD_DIM, HEAD_DIM), F32),
                       gdn=jnp.zeros((bp, N_HEADS, HEAD_DIM, HEAD_DIM), F32),
                       conv=jnp.zeros((bp, CONV_K - 1, 3 * BR_W), F32))

    yp = x_prompt
    ys = jnp.pad(x_sample, ((0, 0), (0, tpad - dec_seq), (0, 0)))
    outs = [[] for _ in range(10)]
    for l in range(depth):
        lw = dict(
            f1_pre=row(ffn1_norm_pre[l]), f1_post=row(ffn1_norm_post[l]), f1_wg=f1[0][l], f1_wu=f1[1][l], f1_wd=f1[2][l],
            m_pre=row(mix_norm_pre[l]), m_post=row(mix_norm_post[l]), w_in=w_in_r[l], lb=row(lbs[l]),
            norm_a=row(hgrn_norm[l]), conv_w=gdn_conv_w[l], alog=alog_p[l:l + 1], dtb=dtb_p[l:l + 1],
            norm_c=row(gdn_norm[l]), w_cat=w_cat[l],
            f2_pre=row(ffn2_norm_pre[l]), f2_post=row(ffn2_norm_post[l]), f2_wg=f2[0][l], f2_wu=f2[1][l], f2_wd=f2[2][l],
        )
        yp, pp, sa, sc = _stream_layer(
            yp, lw, zero_states, lambda p: _sb_prompt(p, bias_prompt[l]),
            tm=1024, tf=512, tn=512, tb=256, t_valid=256)
        sample_states = dict(hgrn=state_hgrn[l], gdn=state_gdn[l], conv=state_conv[l])
        ys, ps, sas, scs = _stream_layer(
            ys, lw, sample_states,
            lambda p: _sb_sample(p, cache_k2, cache_v2, page_table, bias_sample[l:l + 1], l, t_valid=dec_seq),
            tm=db * tpad, tf=512, tn=512, tb=tpad, t_valid=dec_seq)

        def cols(p, blk, width):
            return p[:, :, blk * LANES:blk * LANES + width]

        keep = CONV_K - 1
        vals = (sa, sas, sc, scs,
                cols(pp, CQ_BLK, 3 * BR_W)[:, seq - keep:seq],
                cols(ps, CQ_BLK, 3 * BR_W)[:, dec_seq - keep:dec_seq],
                cols(pp, BK_BLK, BR_W).reshape(bp, seq, N_HEADS, HEAD_DIM),
                cols(pp, BV_BLK, BR_W).reshape(bp, seq, N_HEADS, HEAD_DIM),
                cols(ps, BK_BLK, BR_W)[:, :dec_seq].reshape(db, dec_seq, N_HEADS, HEAD_DIM),
                cols(ps, BV_BLK, BR_W)[:, :dec_seq].reshape(db, dec_seq, N_HEADS, HEAD_DIM))
        for acc, v in zip(outs, vals):
            acc.append(v)

    stacked = [jnp.stack(v) for v in outs]
    return (yp, ys[:, :dec_seq], *stacked)
```

```python
import functools

import jax
import jax.numpy as jnp
from jax import lax
from jax.experimental import pallas as pl
from jax.experimental.pallas import tpu as pltpu

F32 = jnp.float32
BF16 = jnp.bfloat16

D_MODEL = 2048
D_FF = 5632
N_HEADS = 8
HEAD_DIM = 128
BR_W = N_HEADS * HEAD_DIM
CONV_K = 4
RMS_EPS = 1e-6
LANES = 128
SUBLANES = 8
VMEM_LIMIT = 56 * 1024 * 1024

GATE_BLK = 0
AQ_BLK, AF_BLK, AI_BLK, AG_BLK = 48, 56, 64, 72
BQ_BLK, BK_BLK, BV_BLK = 80, 88, 96
CQ_BLK, CK_BLK, CV_BLK = 104, 112, 120
CZ_BLK = 128
AB_BLK = 136
PROJ_W = 140 * LANES

HGRN_CHUNK = 32
GDN_CHUNK = 64
SB_BLOCK = 256
SB_HEADS_PER_STEP = 4
PAGE = 128
PAGES_PER_STEP = 8


def _cparams(semantics):
    return pltpu.CompilerParams(dimension_semantics=semantics, vmem_limit_bytes=VMEM_LIMIT)


def _rms(x, g):
    ms = jnp.mean(x * x, axis=-1, keepdims=True)
    return x * lax.rsqrt(ms + RMS_EPS) * g


def _silu(x):
    return x * jax.nn.sigmoid(x)


def _dot(a, b):
    return jnp.dot(a.astype(BF16), b.astype(BF16), preferred_element_type=F32)


def _dot_nt(a, b):
    return lax.dot_general(a.astype(BF16), b.astype(BF16), (((1,), (1,)), ((), ())),
                           preferred_element_type=F32)


def _dot_tn(a, b):
    return lax.dot_general(a.astype(BF16), b.astype(BF16), (((0,), (0,)), ((), ())),
                           preferred_element_type=F32)


def _split2(x):
    hi = x.astype(BF16)
    return hi, (x - hi.astype(F32)).astype(BF16)


def _sum_lhs(m_bf16, x):
    hi, lo = _split2(x)
    d = functools.partial(jnp.dot, preferred_element_type=F32)
    return d(m_bf16, hi) + d(m_bf16, lo)


def _sum_rhs(x, m_twice_bf16):
    return jnp.dot(jnp.concatenate(_split2(x), axis=1), m_twice_bf16, preferred_element_type=F32)


def _iota2(shape, dim):
    return lax.broadcasted_iota(jnp.int32, shape, dim)


def _head(h):
    return slice(h * HEAD_DIM, (h + 1) * HEAD_DIM)


def _ffn_body(x_ref, pre_ref, post_ref, wg_ref, wu_ref, wd_ref, o_ref, xn_ref, acc_ref):
    j = pl.program_id(1)

    @pl.when(j == 0)
    def _():
        xn_ref[...] = _rms(x_ref[...], pre_ref[...]).astype(BF16)
        acc_ref[...] = jnp.zeros_like(acc_ref)

    xn = xn_ref[...]
    g = jnp.dot(xn, wg_ref[...], preferred_element_type=F32)
    u = jnp.dot(xn, wu_ref[...], preferred_element_type=F32)
    acc_ref[...] += _dot(_silu(g) * u, wd_ref[...])

    @pl.when(j == pl.num_programs(1) - 1)
    def _():
        o_ref[...] = x_ref[...] + 0.5 * _rms(acc_ref[...], post_ref[...])


def _ffn(x, pre, post, wg, wu, wd, *, tm, tf):
    m = x.shape[0]
    return pl.pallas_call(
        _ffn_body,
        grid=(m // tm, D_FF // tf),
        in_specs=[
            pl.BlockSpec((tm, D_MODEL), lambda i, j: (i, 0)),
            pl.BlockSpec((1, D_MODEL), lambda i, j: (0, 0)),
            pl.BlockSpec((1, D_MODEL), lambda i, j: (0, 0)),
            pl.BlockSpec((D_MODEL, tf), lambda i, j: (0, j)),
            pl.BlockSpec((D_MODEL, tf), lambda i, j: (0, j)),
            pl.BlockSpec((tf, D_MODEL), lambda i, j: (j, 0)),
        ],
        out_specs=pl.BlockSpec((tm, D_MODEL), lambda i, j: (i, 0)),
        out_shape=jax.ShapeDtypeStruct((m, D_MODEL), F32),
        scratch_shapes=[pltpu.VMEM((tm, D_MODEL), BF16), pltpu.VMEM((tm, D_MODEL), F32)],
        compiler_params=_cparams(("parallel", "arbitrary")),
        name="ffn",
    )(x, pre, post, wg, wu, wd)


def _inproj_body(x_ref, pre_ref, w_ref, kin_ref, vin_ref, o_ref, k_ref, v_ref, xn_ref, *, tm, tn):
    del kin_ref, vin_ref
    j = pl.program_id(1)

    @pl.when(j == 0)
    def _():
        xn_ref[...] = _rms(x_ref[...], pre_ref[...]).astype(BF16)

    o_ref[...] = jnp.dot(xn_ref[...], w_ref[...], preferred_element_type=F32)

    heads_per_tile = tn // HEAD_DIM
    for dst_ref, blk in ((k_ref, BK_BLK), (v_ref, BV_BLK)):
        for q in range(N_HEADS // heads_per_tile):
            @pl.when(j == blk // heads_per_tile + q)
            def _(dst_ref=dst_ref, q=q):
                for hh in range(heads_per_tile):
                    dst_ref[pl.ds(q * heads_per_tile + hh, tm, stride=N_HEADS), :] = o_ref[:, _head(hh)]


def _inproj(x, pre, w, k_all, v_all, layer, *, tm, tn):
    m = x.shape[0]
    assert tn % HEAD_DIM == 0 and BR_W % tn == 0 and (BK_BLK * LANES) % tn == 0 and (BV_BLK * LANES) % tn == 0
    kv_spec = pl.BlockSpec((None, tm * N_HEADS, HEAD_DIM), lambda i, j: (layer, i, 0))
    return pl.pallas_call(
        functools.partial(_inproj_body, tm=tm, tn=tn),
        grid=(m // tm, PROJ_W // tn),
        in_specs=[
            pl.BlockSpec((tm, D_MODEL), lambda i, j: (i, 0)),
            pl.BlockSpec((1, D_MODEL), lambda i, j: (0, 0)),
            pl.BlockSpec((D_MODEL, tn), lambda i, j: (0, j)),
            pl.BlockSpec(memory_space=pl.ANY),
            pl.BlockSpec(memory_space=pl.ANY),
        ],
        out_specs=[pl.BlockSpec((tm, tn), lambda i, j: (i, j)), kv_spec, kv_spec],
        out_shape=[jax.ShapeDtypeStruct((m, PROJ_W), F32),
                   jax.ShapeDtypeStruct(k_all.shape, F32), jax.ShapeDtypeStruct(v_all.shape, F32)],
        input_output_aliases={3: 1, 4: 2},
        scratch_shapes=[pltpu.VMEM((tm, D_MODEL), BF16)],
        compiler_params=_cparams(("parallel", "arbitrary")),
        name="inproj",
    )(x, pre, w, k_all, v_all)


def _regroup_body(w_ref, o_ref):
    z_end = AB_BLK * LANES - 3 * D_MODEL
    ab_end = z_end + 2 * N_HEADS
    in_w = w_ref.shape[-1]
    gates = in_w - ab_end
    o_ref[:, 0:gates] = w_ref[:, ab_end:in_w].astype(BF16)
    o_ref[:, gates:gates + z_end] = w_ref[:, 0:z_end].astype(BF16)
    o_ref[:, gates + z_end:in_w] = w_ref[:, z_end:ab_end].astype(BF16)
    o_ref[:, in_w:PROJ_W] = jnp.zeros((o_ref.shape[0], PROJ_W - in_w), BF16)


def _regroup_w_in(w_in, *, tr=128):
    depth, d, in_w = w_in.shape
    assert in_w == AB_BLK * LANES + 2 * N_HEADS and d % tr == 0
    return pl.pallas_call(
        _regroup_body,
        grid=(depth, d // tr),
        in_specs=[pl.BlockSpec((None, tr, in_w), lambda l, i: (l, i, 0))],
        out_specs=pl.BlockSpec((None, tr, PROJ_W), lambda l, i: (l, i, 0)),
        out_shape=jax.ShapeDtypeStruct((depth, d, PROJ_W), BF16),
        compiler_params=_cparams(("parallel", "parallel")),
        name="regroup_w_in",
    )(w_in)


def _merge_body(oa_ref, ob_ref, oc_ref, gate_ref, w_ref, post_ref, h_ref, out_ref, acc_ref, y_ref):
    j = pl.program_id(1)

    def branch(o_ref, first):
        m = jnp.dot(o_ref[...], w_ref[...], preferred_element_type=F32) * jax.nn.sigmoid(gate_ref[...])
        if first:
            acc_ref[...] = m
        else:
            acc_ref[...] += m

    pl.when(j == 0)(lambda: branch(oa_ref, True))
    pl.when(j == 1)(lambda: branch(ob_ref, False))
    pl.when(j == 2)(lambda: branch(oc_ref, False))

    @pl.when(j == 3)
    def _():
        y_ref[...] = _dot(acc_ref[:, :BR_W], w_ref[...])

    @pl.when(j == 4)
    def _():
        y = y_ref[...] + _dot(acc_ref[:, BR_W:], w_ref[...])
        out_ref[...] = h_ref[...] + _rms(y, post_ref[...])


def _merge(oa, ob, oc, proj, w_cat, post, h, *, tm):
    m = h.shape[0]
    o_spec = pl.BlockSpec((tm, BR_W), lambda i, j: (i, 0))
    return pl.pallas_call(
        _merge_body,
        grid=(m // tm, 5),
        in_specs=[
            o_spec, o_spec, o_spec,
            pl.BlockSpec((tm, D_MODEL), lambda i, j: (i, jnp.minimum(j, 2))),
            pl.BlockSpec((BR_W, D_MODEL), lambda i, j: (j, 0)),
            pl.BlockSpec((1, D_MODEL), lambda i, j: (0, 0)),
            pl.BlockSpec((tm, D_MODEL), lambda i, j: (i, 0)),
        ],
        out_specs=pl.BlockSpec((tm, D_MODEL), lambda i, j: (i, 0)),
        out_shape=jax.ShapeDtypeStruct((m, D_MODEL), F32),
        scratch_shapes=[pltpu.VMEM((tm, D_MODEL), F32), pltpu.VMEM((tm, D_MODEL), F32)],
        compiler_params=_cparams(("parallel", "arbitrary")),
        name="merge",
    )(oa, ob, oc, proj, w_cat, post, h)


def _hgrn_body(q_ref, f_ref, i_ref, g_ref, lb_ref, norm_ref, s0_ref, o_ref, sout_ref, st_ref,
               *, tb, chunk, t_valid):
    tt = pl.program_id(1)

    @pl.when(tt == 0)
    def _():
        for h in range(N_HEADS):
            st_ref[h] = s0_ref[h].T

    lb = lb_ref[...]
    one_m_lb = 1.0 - lb
    norm = norm_ref[...]
    incl = _iota2((chunk, chunk), 0) >= _iota2((chunk, chunk), 1)
    tril = incl.astype(BF16)
    scale = HEAD_DIM ** -0.5
    states = [st_ref[h] for h in range(N_HEADS)]

    for ci in range(tb // chunk):
        rows = slice(ci * chunk, (ci + 1) * chunk)
        sig = jax.nn.sigmoid(f_ref[rows, :])
        log_f = jnp.log(lb + one_m_lb * sig)
        k = one_m_lb * (1.0 - sig)
        if t_valid < tb:
            ok = (_iota2((chunk, 1), 0) + ci * chunk) < t_valid
            log_f = jnp.where(ok, log_f, 0.0)
            k = jnp.where(ok, k, 0.0)
        b = _sum_lhs(tril, log_f)
        qh = _silu(q_ref[rows, :]) * scale * jnp.exp(b)
        kh = k * jnp.exp(-b)
        b_last = b[chunk - 1:chunk, :]
        kd = k * jnp.exp(b_last - b)
        dec = jnp.exp(b_last)
        v = i_ref[rows, :].astype(BF16)
        gate = _silu(g_ref[rows, :])
        qh, kh, kd = qh.astype(BF16), kh.astype(BF16), kd.astype(BF16)
        heads = range(N_HEADS)
        attn = [_dot_nt(qh[:, _head(h)], kh[:, _head(h)]) for h in heads]
        inter = [_dot_nt(qh[:, _head(h)], states[h]) for h in heads]
        kv = [_dot_tn(v[:, _head(h)], kd[:, _head(h)]) for h in heads]
        o = [inter[h] + _dot(jnp.where(incl, attn[h], 0.0), v[:, _head(h)]) for h in heads]
        states = [states[h] * dec[:, _head(h)] + kv[h] for h in heads]
        for h in heads:
            o_ref[rows, _head(h)] = (_rms(o[h], norm) * gate[:, _head(h)]).astype(o_ref.dtype)

    for h in range(N_HEADS):
        st_ref[h] = states[h]

    @pl.when(tt == pl.num_programs(1) - 1)
    def _():
        for h in range(N_HEADS):
            sout_ref[h] = states[h].T


def _hgrn(proj, lb, norm, s0, *, tb, t_valid):
    bsz, t, _ = proj.shape
    chunk = min(HGRN_CHUNK, tb)

    def pspec(blk):
        return pl.BlockSpec((None, tb, BR_W), lambda b, tt: (b, tt, blk // N_HEADS))

    st_spec = pl.BlockSpec((None, N_HEADS, HEAD_DIM, HEAD_DIM), lambda b, tt: (b, 0, 0, 0))
    return pl.pallas_call(
        functools.partial(_hgrn_body, tb=tb, chunk=chunk, t_valid=t_valid),
        grid=(bsz, t // tb),
        in_specs=[
            pspec(AQ_BLK), pspec(AF_BLK), pspec(AI_BLK), pspec(AG_BLK),
            pl.BlockSpec((1, BR_W), lambda b, tt: (0, 0)),
            pl.BlockSpec((1, LANES), lambda b, tt: (0, 0)),
            st_spec,
        ],
        out_specs=[
            pl.BlockSpec((None, tb, BR_W), lambda b, tt: (b, tt, 0)),
            st_spec,
        ],
        out_shape=[
            jax.ShapeDtypeStruct((bsz, t, BR_W), BF16),
            jax.ShapeDtypeStruct((bsz, N_HEADS, HEAD_DIM, HEAD_DIM), F32),
        ],
        scratch_shapes=[pltpu.VMEM((N_HEADS, HEAD_DIM, HEAD_DIM), F32)],
        compiler_params=_cparams(("parallel", "arbitrary")),
        name="hgrn",
    )(proj, proj, proj, proj, lb, norm, s0)


def _gdn_body(q_ref, k_ref, v_ref, z_ref, ab_ref, w_ref, c0_ref, alog_ref, dt_ref, norm_ref, s0_ref,
              o_ref, sout_ref, s_ref, xq_ref, xk_ref, xv_ref, *, tb, chunk, t_valid):
    tt = pl.program_id(1)
    pad = SUBLANES
    keep = CONV_K - 1
    xrefs = (xq_ref, xk_ref, xv_ref)

    @pl.when(tt == 0)
    def _():
        s_ref[...] = s0_ref[...]
        for n, x_ref in enumerate(xrefs):
            x_ref[pad - keep:pad, :] = c0_ref[:, n * BR_W:(n + 1) * BR_W]

    for n, (x_ref, src_ref) in enumerate(zip(xrefs, (q_ref, k_ref, v_ref))):
        w = w_ref[:, n * BR_W:(n + 1) * BR_W]
        x_ref[pad:pad + tb, :] = src_ref[...]
        y = x_ref[pad:pad + tb, :] * w[keep:keep + 1, :]
        for j in range(keep):
            y = y + x_ref[pad - keep + j:pad - keep + j + tb, :] * w[j:j + 1, :]
        x_ref[pad - keep:pad, :] = x_ref[pad + tb - keep:pad + tb, :]
        x_ref[pad:pad + tb, :] = _silu(y)

    neg_a = -jnp.exp(alog_ref[...])
    norm = norm_ref[...]
    row = _iota2((chunk, chunk), 0)
    col = _iota2((chunk, chunk), 1)
    incl = row >= col
    strict = row > col
    tril = incl.astype(BF16)
    eye = (row == col).astype(F32)
    states = [s_ref[h] for h in range(N_HEADS)]
    n_chunks = tb // chunk
    chains = [(ci, h) for ci in range(n_chunks) for h in range(N_HEADS)]

    zg, c = [], {}
    for ci in range(n_chunks):
        rows = slice(ci * chunk, (ci + 1) * chunk)
        xrows = slice(pad + ci * chunk, pad + (ci + 1) * chunk)
        ab = ab_ref[rows, :]
        log_a_all = neg_a * jax.nn.softplus(ab + dt_ref[...])
        beta_all = jax.nn.sigmoid(ab)
        if t_valid < tb:
            ok = (_iota2((chunk, 1), 0) + ci * chunk) < t_valid
            log_a_all = jnp.where(ok, log_a_all, 0.0)
            beta_all = jnp.where(ok, beta_all, 0.0)
        g_all = _sum_lhs(tril, log_a_all)
        g_rows = g_all.T
        eg_all = jnp.exp(g_all)
        zg.append(_silu(z_ref[rows, :]))
        for h in range(N_HEADS):
            cs = _head(h)
            qc, kc, vc = xq_ref[xrows, cs], xk_ref[xrows, cs], xv_ref[xrows, cs]
            qn = qc * lax.rsqrt(jnp.sum(qc * qc, axis=-1, keepdims=True) + 1e-6) * (HEAD_DIM ** -0.5)
            kn = kc * lax.rsqrt(jnp.sum(kc * kc, axis=-1, keepdims=True) + 1e-6)
            beta = beta_all[:, N_HEADS + h:N_HEADS + h + 1]
            g = g_all[:, h:h + 1]
            eg = eg_all[:, h:h + 1]
            g_last = g[chunk - 1:chunk, :]
            c[ci, h] = dict(
                qn=qn.astype(BF16), kn=kn.astype(BF16), beta=beta,
                gam=jnp.where(incl, jnp.exp(g - g_rows[h:h + 1, :]), 0.0),
                rhs=jnp.concatenate([vc * beta, kn * (beta * eg)], axis=1).astype(BF16),
                q_eg=(qn * eg).astype(BF16), k_dec=(kn * jnp.exp(g_last - g)).astype(BF16),
                s_dec=jnp.exp(g_last))

    kk = {ch: _dot_nt(c[ch]["kn"], c[ch]["kn"]) for ch in chains}
    qk = {ch: _dot_nt(c[ch]["qn"], c[ch]["kn"]) for ch in chains}
    a = {ch: jnp.where(strict, kk[ch] * c[ch]["gam"] * c[ch]["beta"], 0.0) for ch in chains}
    aqk = {ch: (qk[ch] * c[ch]["gam"]).astype(BF16) for ch in chains}
    t_inv = {ch: eye - jnp.where(row // 2 == col // 2, a[ch], 0.0) for ch in chains}
    size = 2
    while size < chunk:
        below = (row // (2 * size) == col // (2 * size)) & (row // size > col // size)
        at = {ch: _dot(jnp.where(below, a[ch], 0.0), t_inv[ch]) for ch in chains}
        t_inv = {ch: t_inv[ch] - _dot(t_inv[ch], at[ch]) for ch in chains}
        size *= 2
    uw = {ch: _dot(t_inv[ch], c[ch]["rhs"]) for ch in chains}

    for ci in range(n_chunks):
        rows = slice(ci * chunk, (ci + 1) * chunk)
        heads = range(N_HEADS)
        ws = [_dot(jnp.concatenate([uw[ci, h][:, HEAD_DIM:].astype(BF16), c[ci, h]["q_eg"]], axis=0), states[h])
              for h in heads]
        v_new = [(uw[ci, h][:, :HEAD_DIM] - ws[h][:chunk]).astype(BF16) for h in heads]
        o = [ws[h][chunk:] + _dot(aqk[ci, h], v_new[h]) for h in heads]
        states = [states[h] * c[ci, h]["s_dec"] + _dot_tn(c[ci, h]["k_dec"], v_new[h]) for h in heads]
        for h in heads:
            o_ref[rows, _head(h)] = (_rms(o[h], norm) * zg[ci][:, _head(h)]).astype(o_ref.dtype)

    for h in range(N_HEADS):
        s_ref[h] = states[h]

    @pl.when(tt == pl.num_programs(1) - 1)
    def _():
        for h in range(N_HEADS):
            sout_ref[h] = states[h]


def _gdn(proj, conv_w, conv0, alog, dtb, norm, s0, *, tb, t_valid):
    bsz, t, _ = proj.shape
    chunk = min(GDN_CHUNK, tb)
    keep = CONV_K - 1

    def pspec(blk):
        return pl.BlockSpec((None, tb, BR_W), lambda b, tt: (b, tt, blk // N_HEADS))

    row_spec = pl.BlockSpec((1, LANES), lambda b, tt: (0, 0))
    st_spec = pl.BlockSpec((None, N_HEADS, HEAD_DIM, HEAD_DIM), lambda b, tt: (b, 0, 0, 0))
    return pl.pallas_call(
        functools.partial(_gdn_body, tb=tb, chunk=chunk, t_valid=t_valid),
        grid=(bsz, t // tb),
        in_specs=[
            pspec(CQ_BLK), pspec(CK_BLK), pspec(CV_BLK), pspec(CZ_BLK),
            pl.BlockSpec((None, tb, LANES), lambda b, tt: (b, tt, AB_BLK)),
            pl.BlockSpec((CONV_K, 3 * BR_W), lambda b, tt: (0, 0)),
            pl.BlockSpec((None, keep, 3 * BR_W), lambda b, tt: (b, 0, 0)),
            row_spec, row_spec, row_spec,
            st_spec,
        ],
        out_specs=[
            pl.BlockSpec((None, tb, BR_W), lambda b, tt: (b, tt, 0)),
            st_spec,
        ],
        out_shape=[
            jax.ShapeDtypeStruct((bsz, t, BR_W), BF16),
            jax.ShapeDtypeStruct((bsz, N_HEADS, HEAD_DIM, HEAD_DIM), F32),
        ],
        scratch_shapes=[pltpu.VMEM((N_HEADS, HEAD_DIM, HEAD_DIM), F32)]
        + [pltpu.VMEM((tb + SUBLANES, BR_W), F32)] * 3,
        compiler_params=_cparams(("parallel", "arbitrary")),
        name="gdn",
    )(proj, proj, proj, proj, proj, conv_w, conv0, alog, dtb, norm, s0)


LOG2E = 1.4426950408889634


def _log2_sigmoid_pair(z2):
    soft = jnp.log2(1.0 + jnp.exp2(-jnp.abs(z2)))
    ls = jnp.minimum(z2, 0.0) - soft
    return ls, ls - z2


def _sbp_body(q_ref, k_ref, v_ref, bias_ref, o_ref, acc_ref, run_ref, *, blk, n_hd):
    qi = pl.program_id(2)
    heads = range(n_hd)
    q = [q_ref[:, _head(h)].astype(BF16) for h in heads]
    bias = [bias_ref[h, 0:1, 0:1] * LOG2E for h in heads]
    scale = HEAD_DIM ** -0.5 * LOG2E
    row = _iota2((blk, blk), 0)
    col = _iota2((blk, blk), 1)
    newer = (row > col).astype(BF16)
    newer = jnp.concatenate([newer, newer], axis=0)
    earlier = col < row

    def key_block(kb, diagonal):
        rows = pl.ds(pl.multiple_of(kb * blk, blk), blk)
        zs = [_dot_nt(q[h], k_ref[rows, _head(h)]) * scale + bias[h] for h in heads]
        pairs = [_log2_sigmoid_pair(z) for z in zs]
        lfs = [jnp.where(earlier, lf, 0.0) if diagonal else lf for _, lf in pairs]
        afters = [_sum_rhs(lf, newer) for lf in lfs]
        for h in heads:
            w = jnp.exp2(pairs[h][0] + afters[h] + run_ref[h])
            if diagonal:
                w = jnp.where(earlier, w, 0.0)
            acc_ref[:, _head(h)] += _dot(w, v_ref[rows, _head(h)])
            run_ref[h] += afters[h][:, 0:1] + lfs[h][:, 0:1]

    acc_ref[...] = jnp.zeros_like(acc_ref)
    run_ref[...] = jnp.zeros_like(run_ref)
    key_block(qi, True)

    def older(i, carry):
        key_block(qi - i, False)
        return carry

    lax.fori_loop(1, qi + 1, older, 0)
    o_ref[...] = acc_ref[...].astype(o_ref.dtype)


def _sb_prompt(proj, bias_rows):
    bsz, t, _ = proj.shape
    blk = min(SB_BLOCK, t)
    n_hd = SB_HEADS_PER_STEP
    wide = n_hd * HEAD_DIM
    kv_spec = lambda base: pl.BlockSpec((None, t, wide), lambda b, hp, qi: (b, 0, base // n_hd + hp))
    return pl.pallas_call(
        functools.partial(_sbp_body, blk=blk, n_hd=n_hd),
        grid=(bsz, N_HEADS // n_hd, t // blk),
        in_specs=[
            pl.BlockSpec((None, blk, wide), lambda b, hp, qi: (b, qi, BQ_BLK // n_hd + hp)),
            kv_spec(BK_BLK), kv_spec(BV_BLK),
            pl.BlockSpec((n_hd, 1, LANES), lambda b, hp, qi: (hp, 0, 0)),
        ],
        out_specs=pl.BlockSpec((None, blk, wide), lambda b, hp, qi: (b, qi, hp)),
        out_shape=jax.ShapeDtypeStruct((bsz, t, BR_W), BF16),
        scratch_shapes=[pltpu.VMEM((blk, wide), F32), pltpu.VMEM((n_hd, blk, 1), F32)],
        compiler_params=_cparams(("parallel", "parallel", "arbitrary")),
        name="sb_prompt",
    )(proj, proj, proj, bias_rows)


def _sbs_body(pt_ref, q_ref, kn_ref, vn_ref, *rest, tpad, t_valid, n_pg):
    kp_refs, vp_refs = rest[:n_pg], rest[n_pg:2 * n_pg]
    bias_ref, o_ref, qbd_ref, acc_ref, run_ref, knew_ref, vnew_ref = rest[2 * n_pg:]
    j = pl.program_id(1)
    scale = HEAD_DIM ** -0.5 * LOG2E
    newer = (_iota2((PAGE, PAGE), 1) > _iota2((PAGE, PAGE), 0)).astype(BF16)

    def pages(ks, vs, run, mask):
        bias = bias_ref[...] * LOG2E
        zs = [_dot_nt(k, qbd_ref[...]) * scale + bias for k in ks]
        pairs = [_log2_sigmoid_pair(z) for z in zs]
        lfs = [lf if mask is None else jnp.where(mask, lf, 0.0) for _, lf in pairs]
        afters = [_sum_lhs(newer, lf) for lf in lfs]
        total = None
        for (ls, _), lf, after, v in zip(pairs, lfs, afters, vs):
            w = jnp.exp2(ls + after + run)
            if mask is not None:
                w = jnp.where(mask, w, 0.0)
            part = _dot_tn(w, v)
            total = part if total is None else total + part
            run = run + after[0:1, :] + lf[0:1, :]
        return total, run

    def heads_on_lanes(ref):
        return jnp.concatenate([ref[pl.ds(h, PAGE, stride=N_HEADS), :] for h in range(N_HEADS)],
                               axis=1).astype(BF16)

    @pl.when(j == 0)
    def _():
        q = q_ref[...]
        tiled = jnp.concatenate([q] * (LANES // tpad), axis=0)
        r = _iota2((LANES, BR_W), 0)
        c = _iota2((LANES, BR_W), 1)
        qbd_ref[...] = jnp.where(r // tpad == c // HEAD_DIM, tiled, 0.0).astype(BF16)
        knew_ref[...] = jnp.zeros_like(knew_ref)
        vnew_ref[...] = jnp.zeros_like(vnew_ref)
        knew_ref[0:tpad, :] = kn_ref[...].astype(BF16)
        vnew_ref[0:tpad, :] = vn_ref[...].astype(BF16)
        row = _iota2((PAGE, LANES), 0)
        col = _iota2((PAGE, LANES), 1)
        mask = (row < col % tpad) & (row < t_valid)
        acc, run = pages([knew_ref[...]], [vnew_ref[...]], jnp.zeros((1, LANES), F32), mask)
        acc_ref[...] = acc
        run_ref[...] = run

    @pl.when(j > 0)
    def _():
        total, run = pages([heads_on_lanes(r) for r in kp_refs], [heads_on_lanes(r) for r in vp_refs],
                           run_ref[...], None)
        acc_ref[...] += total
        run_ref[...] = run

    @pl.when(j == pl.num_programs(1) - 1)
    def _():
        for hh in range(N_HEADS):
            o_ref[:, _head(hh)] = acc_ref[hh * tpad:(hh + 1) * tpad, _head(hh)].astype(o_ref.dtype)


def _sb_sample(proj, cache_k, cache_v, page_table, bias_row, layer, *, t_valid):
    bsz, tpad, _ = proj.shape
    n_pages = page_table.shape[1]
    n_pg = PAGES_PER_STEP
    assert N_HEADS * tpad <= LANES and cache_k.shape[2:] == (PAGE * N_HEADS, HEAD_DIM) and n_pages % n_pg == 0

    def pspec(blk):
        return pl.BlockSpec((None, tpad, BR_W), lambda b, j, pt: (b, 0, blk // N_HEADS))

    def cache_spec(g):
        def index(b, j, pt):
            return (layer, pt[b, n_pages - 1 - (jnp.maximum(j, 1) - 1) * n_pg - g], 0, 0)
        return pl.BlockSpec((None, None, PAGE * N_HEADS, HEAD_DIM), index)

    return pl.pallas_call(
        functools.partial(_sbs_body, tpad=tpad, t_valid=t_valid, n_pg=n_pg),
        grid_spec=pltpu.PrefetchScalarGridSpec(
            num_scalar_prefetch=1,
            grid=(bsz, n_pages // n_pg + 1),
            in_specs=[pspec(BQ_BLK), pspec(BK_BLK), pspec(BV_BLK)]
            + [cache_spec(g) for g in range(n_pg)] * 2
            + [pl.BlockSpec((1, LANES), lambda b, j, pt: (0, 0))],
            out_specs=pl.BlockSpec((None, tpad, BR_W), lambda b, j, pt: (b, 0, 0)),
            scratch_shapes=[
                pltpu.VMEM((LANES, BR_W), BF16),
                pltpu.VMEM((LANES, BR_W), F32),
                pltpu.VMEM((1, LANES), F32),
                pltpu.VMEM((PAGE, BR_W), BF16),
                pltpu.VMEM((PAGE, BR_W), BF16),
            ],
        ),
        out_shape=jax.ShapeDtypeStruct((bsz, tpad, BR_W), BF16),
        compiler_params=_cparams(("parallel", "arbitrary")),
        name="sb_sample",
    )(page_table, proj, proj, proj, *([cache_k] * n_pg), *([cache_v] * n_pg), bias_row)


def _pad_lanes(v):
    return jnp.pad(v.astype(F32), ((0, 0), (0, LANES - v.shape[-1])))


def _stream_layer(x, lw, states, kv, layer, attn_fn, *, tm, tf, tn, tb, t_valid):
    bsz, t, _ = x.shape
    m = bsz * t
    x2 = x.reshape(m, D_MODEL)
    h = _ffn(x2, lw["f1_pre"], lw["f1_post"], lw["f1_wg"], lw["f1_wu"], lw["f1_wd"], tm=min(tm, 512), tf=tf)
    proj, k_all, v_all = _inproj(h, lw["m_pre"], lw["w_in"], kv[0], kv[1], layer, tm=tm, tn=tn)
    proj = proj.reshape(bsz, t, PROJ_W)
    oa, sa = _hgrn(proj, lw["lb"], lw["norm_a"], states["hgrn"], tb=tb, t_valid=t_valid)
    ob = attn_fn(proj)
    oc, sc = _gdn(proj, lw["conv_w"], states["conv"], lw["alog"], lw["dtb"], lw["norm_c"], states["gdn"],
                  tb=min(tb, 2 * GDN_CHUNK), t_valid=t_valid)
    h = _merge(oa.reshape(m, BR_W), ob.reshape(m, BR_W), oc.reshape(m, BR_W), proj.reshape(m, PROJ_W),
               lw["w_cat"], lw["m_post"], h, tm=min(tm, 512))
    y = _ffn(h, lw["f2_pre"], lw["f2_post"], lw["f2_wg"], lw["f2_wu"], lw["f2_wd"], tm=min(tm, 512), tf=tf)
    return y.reshape(bsz, t, D_MODEL), proj, sa, sc, (k_all, v_all)


def kernel(x_prompt, x_sample, cache_k, cache_v, state_hgrn, state_gdn, state_conv, page_table,
           ffn1_norm_pre, ffn1_norm_post, ffn1_w_gate, ffn1_w_up, ffn1_w_down,
           mix_norm_pre, mix_norm_post, w_in, hgrn_lb_raw, hgrn_norm,
           gdn_conv_w, gdn_a_log, gdn_dt_bias, gdn_norm, sb_bias, w_branch, w_out,
           ffn2_norm_pre, ffn2_norm_post, ffn2_w_gate, ffn2_w_up, ffn2_w_down):
    depth = w_in.shape[0]
    bp, seq, _ = x_prompt.shape
    db, dec_seq, _ = x_sample.shape
    tpad = SUBLANES * pl.cdiv(dec_seq, SUBLANES)

    lbs = jnp.cumsum(jax.nn.softmax(hgrn_lb_raw.astype(F32), axis=0), axis=0)
    lbs = lbs - lbs[:1]

    w_in_r = _regroup_w_in(w_in)
    w_cat = jnp.concatenate([w_branch.reshape(depth, 3 * BR_W, D_MODEL), w_out], axis=1).astype(BF16)
    bf = lambda w: w.astype(BF16)
    f1 = (bf(ffn1_w_gate), bf(ffn1_w_up), bf(ffn1_w_down))
    f2 = (bf(ffn2_w_gate), bf(ffn2_w_up), bf(ffn2_w_down))
    cache_k2 = cache_k.reshape(depth, cache_k.shape[1], PAGE * N_HEADS, HEAD_DIM)
    cache_v2 = cache_v.reshape(depth, cache_v.shape[1], PAGE * N_HEADS, HEAD_DIM)
    alog_p, dtb_p = _pad_lanes(gdn_a_log), _pad_lanes(gdn_dt_bias)
    bias_prompt = jnp.broadcast_to(sb_bias.astype(F32)[:, :, None, None], (depth, N_HEADS, 1, LANES))
    bias_sample = _pad_lanes(jnp.repeat(sb_bias.astype(F32), tpad, axis=1))

    row = lambda v: v.reshape(1, -1)
    zero_states = dict(hgrn=jnp.zeros((bp, N_HEADS, HEAD_DIM, HEAD_DIM), F32),
                       gdn=jnp.zeros((bp, N_HEADS, HEAD_DIM, HEAD_DIM), F32),
                       conv=jnp.zeros((bp, CONV_K - 1, 3 * BR_W), F32))

    yp = x_prompt
    ys = jnp.pad(x_sample, ((0, 0), (0, tpad - dec_seq), (0, 0)))
    kv_p = tuple(jnp.zeros((depth, bp * seq * N_HEADS, HEAD_DIM), F32) for _ in range(2))
    kv_s = tuple(jnp.zeros((depth, db * tpad * N_HEADS, HEAD_DIM), F32) for _ in range(2))
    outs = [[] for _ in range(6)]
    for l in range(depth):
        lw = dict(
            f1_pre=row(ffn1_norm_pre[l]), f1_post=row(ffn1_norm_post[l]), f1_wg=f1[0][l], f1_wu=f1[1][l], f1_wd=f1[2][l],
            m_pre=row(mix_norm_pre[l]), m_post=row(mix_norm_post[l]), w_in=w_in_r[l], lb=row(lbs[l]),
            norm_a=row(hgrn_norm[l]), conv_w=gdn_conv_w[l], alog=alog_p[l:l + 1], dtb=dtb_p[l:l + 1],
            norm_c=row(gdn_norm[l]), w_cat=w_cat[l],
            f2_pre=row(ffn2_norm_pre[l]), f2_post=row(ffn2_norm_post[l]), f2_wg=f2[0][l], f2_wu=f2[1][l], f2_wd=f2[2][l],
        )
        yp, pp, sa, sc, kv_p = _stream_layer(
            yp, lw, zero_states, kv_p, l, lambda p: _sb_prompt(p, bias_prompt[l]),
            tm=1024, tf=512, tn=512, tb=256, t_valid=256)
        sample_states = dict(hgrn=state_hgrn[l], gdn=state_gdn[l], conv=state_conv[l])
        ys, ps, sas, scs, kv_s = _stream_layer(
            ys, lw, sample_states, kv_s, l,
            lambda p: _sb_sample(p, cache_k2, cache_v2, page_table, bias_sample[l:l + 1], l, t_valid=dec_seq),
            tm=db * tpad, tf=512, tn=512, tb=tpad, t_valid=dec_seq)

        conv_cols = slice(CQ_BLK * LANES, CQ_BLK * LANES + 3 * BR_W)
        keep = CONV_K - 1
        vals = (sa, sas, sc, scs, pp[:, seq - keep:seq, conv_cols], ps[:, dec_seq - keep:dec_seq, conv_cols])
        for acc, v in zip(outs, vals):
            acc.append(v)

    stacked = [jnp.stack(v) for v in outs]
    kv_prompt = [a.reshape(depth, bp, seq, N_HEADS, HEAD_DIM) for a in kv_p]
    kv_sample = [a.reshape(depth, db, tpad, N_HEADS, HEAD_DIM)[:, :, :dec_seq] for a in kv_s]
    return (yp, ys[:, :dec_seq], *stacked, *kv_prompt, *kv_sample)
```

```python
import functools

import jax
import jax.numpy as jnp
from jax import lax
from jax.experimental import pallas as pl
from jax.experimental.pallas import tpu as pltpu

F32 = jnp.float32
BF16 = jnp.bfloat16

D_MODEL = 2048
D_FF = 5632
N_HEADS = 8
HEAD_DIM = 128
BR_W = N_HEADS * HEAD_DIM
CONV_K = 4
RMS_EPS = 1e-6
LANES = 128
SUBLANES = 8
VMEM_LIMIT = 56 * 1024 * 1024

GATE_BLK = 0
AQ_BLK, AF_BLK, AI_BLK, AG_BLK = 48, 56, 64, 72
BQ_BLK, BK_BLK, BV_BLK = 80, 88, 96
CQ_BLK, CK_BLK, CV_BLK = 104, 112, 120
CZ_BLK = 128
AB_BLK = 136
PROJ_W = 140 * LANES

HGRN_CHUNK = 32
GDN_CHUNK = 64
SB_BLOCK = 256
SB_HEADS_PER_STEP = 4
PAGE = 128
PAGES_PER_STEP = 8


def _cparams(semantics):
    return pltpu.CompilerParams(dimension_semantics=semantics, vmem_limit_bytes=VMEM_LIMIT)


def _rms(x, g):
    ms = jnp.mean(x * x, axis=-1, keepdims=True)
    return x * lax.rsqrt(ms + RMS_EPS) * g


def _silu(x):
    return x * jax.nn.sigmoid(x)


def _dot(a, b):
    return jnp.dot(a.astype(BF16), b.astype(BF16), preferred_element_type=F32)


def _dot_nt(a, b):
    return lax.dot_general(a.astype(BF16), b.astype(BF16), (((1,), (1,)), ((), ())),
                           preferred_element_type=F32)


def _dot_tn(a, b):
    return lax.dot_general(a.astype(BF16), b.astype(BF16), (((0,), (0,)), ((), ())),
                           preferred_element_type=F32)


def _split2(x):
    hi = x.astype(BF16)
    return hi, (x - hi.astype(F32)).astype(BF16)


def _sum_lhs(m_bf16, x):
    hi, lo = _split2(x)
    d = functools.partial(jnp.dot, preferred_element_type=F32)
    return d(m_bf16, hi) + d(m_bf16, lo)


def _sum_rhs(x, m_twice_bf16):
    return jnp.dot(jnp.concatenate(_split2(x), axis=1), m_twice_bf16, preferred_element_type=F32)


def _iota2(shape, dim):
    return lax.broadcasted_iota(jnp.int32, shape, dim)


def _head(h):
    return slice(h * HEAD_DIM, (h + 1) * HEAD_DIM)


def _ffn_body(x_ref, pre_ref, post_ref, wg_ref, wu_ref, wd_ref, o_ref, xn_ref):
    j = pl.program_id(1)

    @pl.when(j == 0)
    def _():
        xn_ref[...] = _rms(x_ref[...], pre_ref[...]).astype(BF16)
        o_ref[...] = jnp.zeros_like(o_ref)

    xn = xn_ref[...]
    g = jnp.dot(xn, wg_ref[...], preferred_element_type=F32)
    u = jnp.dot(xn, wu_ref[...], preferred_element_type=F32)
    o_ref[...] += _dot(_silu(g) * u, wd_ref[...])

    @pl.when(j == pl.num_programs(1) - 1)
    def _():
        o_ref[...] = x_ref[...] + 0.5 * _rms(o_ref[...], post_ref[...])


def _ffn(x, pre, post, wg, wu, wd, layer, *, tm, tf):
    m = x.shape[0]
    return pl.pallas_call(
        _ffn_body,
        grid=(m // tm, D_FF // tf),
        in_specs=[
            pl.BlockSpec((tm, D_MODEL), lambda i, j: (i, 0), pipeline_mode=pl.Buffered(1)),
            pl.BlockSpec((1, D_MODEL), lambda i, j: (0, 0)),
            pl.BlockSpec((1, D_MODEL), lambda i, j: (0, 0)),
            pl.BlockSpec((None, D_MODEL, tf), lambda i, j: (layer, 0, j)),
            pl.BlockSpec((None, D_MODEL, tf), lambda i, j: (layer, 0, j)),
            pl.BlockSpec((None, tf, D_MODEL), lambda i, j: (layer, j, 0)),
        ],
        out_specs=pl.BlockSpec((tm, D_MODEL), lambda i, j: (i, 0)),
        out_shape=jax.ShapeDtypeStruct((m, D_MODEL), F32),
        scratch_shapes=[pltpu.VMEM((tm, D_MODEL), BF16)],
        compiler_params=_cparams(("parallel", "arbitrary")),
        name="ffn",
    )(x, pre, post, wg, wu, wd)


def _inproj_body(x_ref, pre_ref, w_ref, kin_ref, vin_ref, o_ref, k_ref, v_ref, xn_ref, *, tm, tn):
    del kin_ref, vin_ref
    j = pl.program_id(1)

    @pl.when(j == 0)
    def _():
        xn_ref[...] = _rms(x_ref[...], pre_ref[...]).astype(BF16)

    o_ref[...] = _dot_nt(xn_ref[...], w_ref[...])

    heads_per_tile = tn // HEAD_DIM
    for dst_ref, blk in ((k_ref, BK_BLK), (v_ref, BV_BLK)):
        for q in range(N_HEADS // heads_per_tile):
            @pl.when(j == blk // heads_per_tile + q)
            def _(dst_ref=dst_ref, q=q):
                for hh in range(heads_per_tile):
                    dst_ref[pl.ds(q * heads_per_tile + hh, tm, stride=N_HEADS), :] = o_ref[:, _head(hh)]


def _inproj(x, pre, w, k_all, v_all, layer, *, tm, tn):
    m = x.shape[0]
    assert tn % HEAD_DIM == 0 and BR_W % tn == 0 and (BK_BLK * LANES) % tn == 0 and (BV_BLK * LANES) % tn == 0
    kv_spec = pl.BlockSpec((None, tm * N_HEADS, HEAD_DIM), lambda i, j: (layer, i, 0))
    return pl.pallas_call(
        functools.partial(_inproj_body, tm=tm, tn=tn),
        grid=(m // tm, PROJ_W // tn),
        in_specs=[
            pl.BlockSpec((tm, D_MODEL), lambda i, j: (i, 0)),
            pl.BlockSpec((1, D_MODEL), lambda i, j: (0, 0)),
            pl.BlockSpec((None, tn, D_MODEL), lambda i, j: (layer, j, 0)),
            pl.BlockSpec(memory_space=pl.ANY),
            pl.BlockSpec(memory_space=pl.ANY),
        ],
        out_specs=[pl.BlockSpec((tm, tn), lambda i, j: (i, j)), kv_spec, kv_spec],
        out_shape=[jax.ShapeDtypeStruct((m, PROJ_W), F32),
                   jax.ShapeDtypeStruct(k_all.shape, F32), jax.ShapeDtypeStruct(v_all.shape, F32)],
        input_output_aliases={3: 1, 4: 2},
        scratch_shapes=[pltpu.VMEM((tm, D_MODEL), BF16)],
        compiler_params=_cparams(("parallel", "arbitrary")),
        name="inproj",
    )(x, pre, w, k_all, v_all)


def _regroup_w_in(w_in):
    w_t = jnp.swapaxes(w_in, 1, 2)
    z_end = AB_BLK * LANES - 3 * D_MODEL
    ab_end = z_end + 2 * N_HEADS
    assert w_t.shape[1] == AB_BLK * LANES + 2 * N_HEADS
    pad = jnp.zeros((w_t.shape[0], PROJ_W - w_t.shape[1], w_t.shape[2]), w_t.dtype)
    return jnp.concatenate([w_t[:, ab_end:], w_t[:, :z_end], w_t[:, z_end:ab_end], pad], axis=1).astype(BF16)


def _merge_body(oa_ref, ob_ref, oc_ref, gate_ref, w_ref, post_ref, h_ref, out_ref, acc_ref, y_ref):
    j = pl.program_id(1)

    def branch(o_ref, first):
        m = jnp.dot(o_ref[...], w_ref[...], preferred_element_type=F32) * jax.nn.sigmoid(gate_ref[...])
        if first:
            acc_ref[...] = m
        else:
            acc_ref[...] += m

    pl.when(j == 0)(lambda: branch(oa_ref, True))
    pl.when(j == 1)(lambda: branch(ob_ref, False))
    pl.when(j == 2)(lambda: branch(oc_ref, False))

    @pl.when(j == 3)
    def _():
        y_ref[...] = _dot(acc_ref[:, :BR_W], w_ref[...])

    @pl.when(j == 4)
    def _():
        y = y_ref[...] + _dot(acc_ref[:, BR_W:], w_ref[...])
        out_ref[...] = h_ref[...] + _rms(y, post_ref[...])


def _merge(oa, ob, oc, proj, w_cat, post, h, layer, *, tm):
    m = h.shape[0]
    o_spec = pl.BlockSpec((tm, BR_W), lambda i, j: (i, 0))
    return pl.pallas_call(
        _merge_body,
        grid=(m // tm, 5),
        in_specs=[
            o_spec, o_spec, o_spec,
            pl.BlockSpec((tm, D_MODEL), lambda i, j: (i, jnp.minimum(j, 2))),
            pl.BlockSpec((None, BR_W, D_MODEL), lambda i, j: (layer, j, 0)),
            pl.BlockSpec((1, D_MODEL), lambda i, j: (0, 0)),
            pl.BlockSpec((tm, D_MODEL), lambda i, j: (i, 0)),
        ],
        out_specs=pl.BlockSpec((tm, D_MODEL), lambda i, j: (i, 0)),
        out_shape=jax.ShapeDtypeStruct((m, D_MODEL), F32),
        scratch_shapes=[pltpu.VMEM((tm, D_MODEL), F32), pltpu.VMEM((tm, D_MODEL), F32)],
        compiler_params=_cparams(("parallel", "arbitrary")),
        name="merge",
    )(oa, ob, oc, proj, w_cat, post, h)


def _hgrn_body(q_ref, f_ref, i_ref, g_ref, lb_ref, norm_ref, s0_ref, o_ref, sout_ref, st_ref,
               *, tb, chunk, t_valid):
    tt = pl.program_id(1)

    @pl.when(tt == 0)
    def _():
        for h in range(N_HEADS):
            st_ref[h] = s0_ref[h].T

    lb = lb_ref[...]
    one_m_lb = 1.0 - lb
    norm = norm_ref[...]
    incl = _iota2((chunk, chunk), 0) >= _iota2((chunk, chunk), 1)
    tril = incl.astype(BF16)
    scale = HEAD_DIM ** -0.5
    states = [st_ref[h] for h in range(N_HEADS)]

    for ci in range(tb // chunk):
        rows = slice(ci * chunk, (ci + 1) * chunk)
        sig = jax.nn.sigmoid(f_ref[rows, :])
        log_f = jnp.log(lb + one_m_lb * sig)
        k = one_m_lb * (1.0 - sig)
        if t_valid < tb:
            ok = (_iota2((chunk, 1), 0) + ci * chunk) < t_valid
            log_f = jnp.where(ok, log_f, 0.0)
            k = jnp.where(ok, k, 0.0)
        b = _sum_lhs(tril, log_f)
        qh = _silu(q_ref[rows, :]) * scale * jnp.exp(b)
        kh = k * jnp.exp(-b)
        b_last = b[chunk - 1:chunk, :]
        kd = k * jnp.exp(b_last - b)
        dec = jnp.exp(b_last)
        v = i_ref[rows, :].astype(BF16)
        gate = _silu(g_ref[rows, :])
        qh, kh, kd = qh.astype(BF16), kh.astype(BF16), kd.astype(BF16)
        heads = range(N_HEADS)
        attn = [_dot_nt(qh[:, _head(h)], kh[:, _head(h)]) for h in heads]
        inter = [_dot_nt(qh[:, _head(h)], states[h]) for h in heads]
        kv = [_dot_tn(v[:, _head(h)], kd[:, _head(h)]) for h in heads]
        o = [inter[h] + _dot(jnp.where(incl, attn[h], 0.0), v[:, _head(h)]) for h in heads]
        states = [states[h] * dec[:, _head(h)] + kv[h] for h in heads]
        for h in heads:
            o_ref[rows, _head(h)] = (_rms(o[h], norm) * gate[:, _head(h)]).astype(o_ref.dtype)

    for h in range(N_HEADS):
        st_ref[h] = states[h]

    @pl.when(tt == pl.num_programs(1) - 1)
    def _():
        for h in range(N_HEADS):
            sout_ref[h] = states[h].T


def _hgrn(proj, lb, norm, s0, *, tb, t_valid):
    bsz, t, _ = proj.shape
    chunk = min(HGRN_CHUNK, tb)

    def pspec(blk):
        return pl.BlockSpec((None, tb, BR_W), lambda b, tt: (b, tt, blk // N_HEADS))

    st_spec = pl.BlockSpec((None, N_HEADS, HEAD_DIM, HEAD_DIM), lambda b, tt: (b, 0, 0, 0))
    return pl.pallas_call(
        functools.partial(_hgrn_body, tb=tb, chunk=chunk, t_valid=t_valid),
        grid=(bsz, t // tb),
        in_specs=[
            pspec(AQ_BLK), pspec(AF_BLK), pspec(AI_BLK), pspec(AG_BLK),
            pl.BlockSpec((1, BR_W), lambda b, tt: (0, 0)),
            pl.BlockSpec((1, LANES), lambda b, tt: (0, 0)),
            st_spec,
        ],
        out_specs=[
            pl.BlockSpec((None, tb, BR_W), lambda b, tt: (b, tt, 0)),
            st_spec,
        ],
        out_shape=[
            jax.ShapeDtypeStruct((bsz, t, BR_W), BF16),
            jax.ShapeDtypeStruct((bsz, N_HEADS, HEAD_DIM, HEAD_DIM), F32),
        ],
        scratch_shapes=[pltpu.VMEM((N_HEADS, HEAD_DIM, HEAD_DIM), F32)],
        compiler_params=_cparams(("parallel", "arbitrary")),
        name="hgrn",
    )(proj, proj, proj, proj, lb, norm, s0)


def _gdn_body(q_ref, k_ref, v_ref, z_ref, ab_ref, w_ref, c0_ref, alog_ref, dt_ref, norm_ref, s0_ref,
              o_ref, sout_ref, s_ref, xq_ref, xk_ref, xv_ref, *, tb, chunk, t_valid):
    tt = pl.program_id(1)
    pad = SUBLANES
    keep = CONV_K - 1
    xrefs = (xq_ref, xk_ref, xv_ref)

    @pl.when(tt == 0)
    def _():
        s_ref[...] = s0_ref[...]
        for n, x_ref in enumerate(xrefs):
            x_ref[pad - keep:pad, :] = c0_ref[:, n * BR_W:(n + 1) * BR_W]

    for n, (x_ref, src_ref) in enumerate(zip(xrefs, (q_ref, k_ref, v_ref))):
        w = w_ref[:, n * BR_W:(n + 1) * BR_W]
        x_ref[pad:pad + tb, :] = src_ref[...]
        y = x_ref[pad:pad + tb, :] * w[keep:keep + 1, :]
        for j in range(keep):
            y = y + x_ref[pad - keep + j:pad - keep + j + tb, :] * w[j:j + 1, :]
        x_ref[pad - keep:pad, :] = x_ref[pad + tb - keep:pad + tb, :]
        x_ref[pad:pad + tb, :] = _silu(y)

    neg_a = -jnp.exp(alog_ref[...])
    norm = norm_ref[...]
    row = _iota2((chunk, chunk), 0)
    col = _iota2((chunk, chunk), 1)
    incl = row >= col
    strict = row > col
    tril = incl.astype(BF16)
    eye = (row == col).astype(F32)
    states = [s_ref[h] for h in range(N_HEADS)]
    n_chunks = tb // chunk
    chains = [(ci, h) for ci in range(n_chunks) for h in range(N_HEADS)]

    zg, c = [], {}
    for ci in range(n_chunks):
        rows = slice(ci * chunk, (ci + 1) * chunk)
        xrows = slice(pad + ci * chunk, pad + (ci + 1) * chunk)
        ab = ab_ref[rows, :]
        log_a_all = neg_a * jax.nn.softplus(ab + dt_ref[...])
        beta_all = jax.nn.sigmoid(ab)
        if t_valid < tb:
            ok = (_iota2((chunk, 1), 0) + ci * chunk) < t_valid
            log_a_all = jnp.where(ok, log_a_all, 0.0)
            beta_all = jnp.where(ok, beta_all, 0.0)
        g_all = _sum_lhs(tril, log_a_all)
        g_rows = g_all.T
        eg_all = jnp.exp(g_all)
        zg.append(_silu(z_ref[rows, :]))
        for h in range(N_HEADS):
            cs = _head(h)
            qc, kc, vc = xq_ref[xrows, cs], xk_ref[xrows, cs], xv_ref[xrows, cs]
            qn = qc * lax.rsqrt(jnp.sum(qc * qc, axis=-1, keepdims=True) + 1e-6) * (HEAD_DIM ** -0.5)
            kn = kc * lax.rsqrt(jnp.sum(kc * kc, axis=-1, keepdims=True) + 1e-6)
            beta = beta_all[:, N_HEADS + h:N_HEADS + h + 1]
            g = g_all[:, h:h + 1]
            eg = eg_all[:, h:h + 1]
            g_last = g[chunk - 1:chunk, :]
            c[ci, h] = dict(
                qn=qn.astype(BF16), kn=kn.astype(BF16), beta=beta,
                gam=jnp.where(incl, jnp.exp(g - g_rows[h:h + 1, :]), 0.0),
                rhs=jnp.concatenate([vc * beta, kn * (beta * eg)], axis=1).astype(BF16),
                q_eg=(qn * eg).astype(BF16), k_dec=(kn * jnp.exp(g_last - g)).astype(BF16),
                s_dec=jnp.exp(g_last))

    kk = {ch: _dot_nt(c[ch]["kn"], c[ch]["kn"]) for ch in chains}
    qk = {ch: _dot_nt(c[ch]["qn"], c[ch]["kn"]) for ch in chains}
    a = {ch: jnp.where(strict, kk[ch] * c[ch]["gam"] * c[ch]["beta"], 0.0) for ch in chains}
    aqk = {ch: (qk[ch] * c[ch]["gam"]).astype(BF16) for ch in chains}
    t_inv = {ch: eye - jnp.where(row // 2 == col // 2, a[ch], 0.0) for ch in chains}
    size = 2
    while size < chunk:
        below = (row // (2 * size) == col // (2 * size)) & (row // size > col // size)
        at = {ch: _dot(jnp.where(below, a[ch], 0.0), t_inv[ch]) for ch in chains}
        t_inv = {ch: t_inv[ch] - _dot(t_inv[ch], at[ch]) for ch in chains}
        size *= 2
    uw = {ch: _dot(t_inv[ch], c[ch]["rhs"]) for ch in chains}

    for ci in range(n_chunks):
        rows = slice(ci * chunk, (ci + 1) * chunk)
        heads = range(N_HEADS)
        ws = [_dot(jnp.concatenate([uw[ci, h][:, HEAD_DIM:].astype(BF16), c[ci, h]["q_eg"]], axis=0), states[h])
              for h in heads]
        v_new = [(uw[ci, h][:, :HEAD_DIM] - ws[h][:chunk]).astype(BF16) for h in heads]
        o = [ws[h][chunk:] + _dot(aqk[ci, h], v_new[h]) for h in heads]
        states = [states[h] * c[ci, h]["s_dec"] + _dot_tn(c[ci, h]["k_dec"], v_new[h]) for h in heads]
        for h in heads:
            o_ref[rows, _head(h)] = (_rms(o[h], norm) * zg[ci][:, _head(h)]).astype(o_ref.dtype)

    for h in range(N_HEADS):
        s_ref[h] = states[h]

    @pl.when(tt == pl.num_programs(1) - 1)
    def _():
        for h in range(N_HEADS):
            sout_ref[h] = states[h]


def _gdn(proj, conv_w, conv0, alog, dtb, norm, s0, *, tb, t_valid):
    bsz, t, _ = proj.shape
    chunk = min(GDN_CHUNK, tb)
    keep = CONV_K - 1

    def pspec(blk):
        return pl.BlockSpec((None, tb, BR_W), lambda b, tt: (b, tt, blk // N_HEADS))

    row_spec = pl.BlockSpec((1, LANES), lambda b, tt: (0, 0))
    st_spec = pl.BlockSpec((None, N_HEADS, HEAD_DIM, HEAD_DIM), lambda b, tt: (b, 0, 0, 0))
    return pl.pallas_call(
        functools.partial(_gdn_body, tb=tb, chunk=chunk, t_valid=t_valid),
        grid=(bsz, t // tb),
        in_specs=[
            pspec(CQ_BLK), pspec(CK_BLK), pspec(CV_BLK), pspec(CZ_BLK),
            pl.BlockSpec((None, tb, LANES), lambda b, tt: (b, tt, AB_BLK)),
            pl.BlockSpec((CONV_K, 3 * BR_W), lambda b, tt: (0, 0)),
            pl.BlockSpec((None, keep, 3 * BR_W), lambda b, tt: (b, 0, 0)),
            row_spec, row_spec, row_spec,
            st_spec,
        ],
        out_specs=[
            pl.BlockSpec((None, tb, BR_W), lambda b, tt: (b, tt, 0)),
            st_spec,
        ],
        out_shape=[
            jax.ShapeDtypeStruct((bsz, t, BR_W), BF16),
            jax.ShapeDtypeStruct((bsz, N_HEADS, HEAD_DIM, HEAD_DIM), F32),
        ],
        scratch_shapes=[pltpu.VMEM((N_HEADS, HEAD_DIM, HEAD_DIM), F32)]
        + [pltpu.VMEM((tb + SUBLANES, BR_W), F32)] * 3,
        compiler_params=_cparams(("parallel", "arbitrary")),
        name="gdn",
    )(proj, proj, proj, proj, proj, conv_w, conv0, alog, dtb, norm, s0)


LOG2E = 1.4426950408889634


def _log2_sigmoid_pair(z2):
    soft = jnp.log2(1.0 + jnp.exp2(-jnp.abs(z2)))
    ls = jnp.minimum(z2, 0.0) - soft
    return ls, ls - z2


def _sbp_body(q_ref, k_ref, v_ref, bias_ref, o_ref, acc_ref, run_ref, *, blk, n_hd):
    qi = pl.program_id(2)
    heads = range(n_hd)
    q = [q_ref[:, _head(h)].astype(BF16) for h in heads]
    bias = [bias_ref[h, 0:1, 0:1] * LOG2E for h in heads]
    scale = HEAD_DIM ** -0.5 * LOG2E
    row = _iota2((blk, blk), 0)
    col = _iota2((blk, blk), 1)
    newer = (row > col).astype(BF16)
    newer = jnp.concatenate([newer, newer], axis=0)
    earlier = col < row

    def key_block(kb, diagonal):
        rows = pl.ds(pl.multiple_of(kb * blk, blk), blk)
        zs = [_dot_nt(q[h], k_ref[rows, _head(h)]) * scale + bias[h] for h in heads]
        pairs = [_log2_sigmoid_pair(z) for z in zs]
        lfs = [jnp.where(earlier, lf, 0.0) if diagonal else lf for _, lf in pairs]
        afters = [_sum_rhs(lf, newer) for lf in lfs]
        for h in heads:
            w = jnp.exp2(pairs[h][0] + afters[h] + run_ref[h])
            if diagonal:
                w = jnp.where(earlier, w, 0.0)
            acc_ref[:, _head(h)] += _dot(w, v_ref[rows, _head(h)])
            run_ref[h] += afters[h][:, 0:1] + lfs[h][:, 0:1]

    acc_ref[...] = jnp.zeros_like(acc_ref)
    run_ref[...] = jnp.zeros_like(run_ref)
    key_block(qi, True)

    def older(i, carry):
        key_block(qi - i, False)
        return carry

    lax.fori_loop(1, qi + 1, older, 0)
    o_ref[...] = acc_ref[...].astype(o_ref.dtype)


def _sb_prompt(proj, bias_rows):
    bsz, t, _ = proj.shape
    blk = min(SB_BLOCK, t)
    n_hd = SB_HEADS_PER_STEP
    wide = n_hd * HEAD_DIM
    kv_spec = lambda base: pl.BlockSpec((None, t, wide), lambda b, hp, qi: (b, 0, base // n_hd + hp))
    return pl.pallas_call(
        functools.partial(_sbp_body, blk=blk, n_hd=n_hd),
        grid=(bsz, N_HEADS // n_hd, t // blk),
        in_specs=[
            pl.BlockSpec((None, blk, wide), lambda b, hp, qi: (b, qi, BQ_BLK // n_hd + hp)),
            kv_spec(BK_BLK), kv_spec(BV_BLK),
            pl.BlockSpec((n_hd, 1, LANES), lambda b, hp, qi: (hp, 0, 0)),
        ],
        out_specs=pl.BlockSpec((None, blk, wide), lambda b, hp, qi: (b, qi, hp)),
        out_shape=jax.ShapeDtypeStruct((bsz, t, BR_W), BF16),
        scratch_shapes=[pltpu.VMEM((blk, wide), F32), pltpu.VMEM((n_hd, blk, 1), F32)],
        compiler_params=_cparams(("parallel", "parallel", "arbitrary")),
        name="sb_prompt",
    )(proj, proj, proj, bias_rows)


def _sbs_body(pt_ref, q_ref, kn_ref, vn_ref, *rest, tpad, t_valid, n_pg):
    kp_refs, vp_refs = rest[:n_pg], rest[n_pg:2 * n_pg]
    bias_ref, o_ref, qbd_ref, acc_ref, run_ref, knew_ref, vnew_ref = rest[2 * n_pg:]
    j = pl.program_id(1)
    scale = HEAD_DIM ** -0.5 * LOG2E
    newer = (_iota2((PAGE, PAGE), 1) > _iota2((PAGE, PAGE), 0)).astype(BF16)

    def pages(ks, vs, run, mask):
        bias = bias_ref[...] * LOG2E
        zs = [_dot_nt(k, qbd_ref[...]) * scale + bias for k in ks]
        pairs = [_log2_sigmoid_pair(z) for z in zs]
        lfs = [lf if mask is None else jnp.where(mask, lf, 0.0) for _, lf in pairs]
        afters = [_sum_lhs(newer, lf) for lf in lfs]
        total = None
        for (ls, _), lf, after, v in zip(pairs, lfs, afters, vs):
            w = jnp.exp2(ls + after + run)
            if mask is not None:
                w = jnp.where(mask, w, 0.0)
            part = _dot_tn(w, v)
            total = part if total is None else total + part
            run = run + after[0:1, :] + lf[0:1, :]
        return total, run

    def heads_on_lanes(ref):
        return jnp.concatenate([ref[pl.ds(h, PAGE, stride=N_HEADS), :] for h in range(N_HEADS)],
                               axis=1).astype(BF16)

    @pl.when(j == 0)
    def _():
        q = q_ref[...]
        tiled = jnp.concatenate([q] * (LANES // tpad), axis=0)
        r = _iota2((LANES, BR_W), 0)
        c = _iota2((LANES, BR_W), 1)
        qbd_ref[...] = jnp.where(r // tpad == c // HEAD_DIM, tiled, 0.0).astype(BF16)
        knew_ref[...] = jnp.zeros_like(knew_ref)
        vnew_ref[...] = jnp.zeros_like(vnew_ref)
        knew_ref[0:tpad, :] = kn_ref[...].astype(BF16)
        vnew_ref[0:tpad, :] = vn_ref[...].astype(BF16)
        row = _iota2((PAGE, LANES), 0)
        col = _iota2((PAGE, LANES), 1)
        mask = (row < col % tpad) & (row < t_valid)
        acc, run = pages([knew_ref[...]], [vnew_ref[...]], jnp.zeros((1, LANES), F32), mask)
        acc_ref[...] = acc
        run_ref[...] = run

    @pl.when(j > 0)
    def _():
        total, run = pages([heads_on_lanes(r) for r in kp_refs], [heads_on_lanes(r) for r in vp_refs],
                           run_ref[...], None)
        acc_ref[...] += total
        run_ref[...] = run

    @pl.when(j == pl.num_programs(1) - 1)
    def _():
        for hh in range(N_HEADS):
            o_ref[:, _head(hh)] = acc_ref[hh * tpad:(hh + 1) * tpad, _head(hh)].astype(o_ref.dtype)


def _sb_sample(proj, cache_k, cache_v, page_table, bias_row, layer, *, t_valid):
    bsz, tpad, _ = proj.shape
    n_pages = page_table.shape[1]
    n_pg = PAGES_PER_STEP
    assert N_HEADS * tpad <= LANES and cache_k.shape[2:] == (PAGE * N_HEADS, HEAD_DIM) and n_pages % n_pg == 0

    def pspec(blk):
        return pl.BlockSpec((None, tpad, BR_W), lambda b, j, pt: (b, 0, blk // N_HEADS))

    def cache_spec(g):
        def index(b, j, pt):
            return (layer, pt[b, n_pages - 1 - (jnp.maximum(j, 1) - 1) * n_pg - g], 0, 0)
        return pl.BlockSpec((None, None, PAGE * N_HEADS, HEAD_DIM), index)

    return pl.pallas_call(
        functools.partial(_sbs_body, tpad=tpad, t_valid=t_valid, n_pg=n_pg),
        grid_spec=pltpu.PrefetchScalarGridSpec(
            num_scalar_prefetch=1,
            grid=(bsz, n_pages // n_pg + 1),
            in_specs=[pspec(BQ_BLK), pspec(BK_BLK), pspec(BV_BLK)]
            + [cache_spec(g) for g in range(n_pg)] * 2
            + [pl.BlockSpec((1, LANES), lambda b, j, pt: (0, 0))],
            out_specs=pl.BlockSpec((None, tpad, BR_W), lambda b, j, pt: (b, 0, 0)),
            scratch_shapes=[
                pltpu.VMEM((LANES, BR_W), BF16),
                pltpu.VMEM((LANES, BR_W), F32),
                pltpu.VMEM((1, LANES), F32),
                pltpu.VMEM((PAGE, BR_W), BF16),
                pltpu.VMEM((PAGE, BR_W), BF16),
            ],
        ),
        out_shape=jax.ShapeDtypeStruct((bsz, tpad, BR_W), BF16),
        compiler_params=_cparams(("parallel", "arbitrary")),
        name="sb_sample",
    )(page_table, proj, proj, proj, *([cache_k] * n_pg), *([cache_v] * n_pg), bias_row)


def _pad_lanes(v):
    return jnp.pad(v.astype(F32), ((0, 0), (0, LANES - v.shape[-1])))


def _stream_layer(x, lw, states, kv, layer, attn_fn, *, tm, tf, tn, tb, t_valid):
    bsz, t, _ = x.shape
    m = bsz * t
    x2 = x.reshape(m, D_MODEL)
    h = _ffn(x2, lw["f1_pre"], lw["f1_post"], *lw["f1_w"], layer, tm=tm, tf=tf)
    proj, k_all, v_all = _inproj(h, lw["m_pre"], lw["w_in"], kv[0], kv[1], layer, tm=tm, tn=tn)
    proj = proj.reshape(bsz, t, PROJ_W)
    oa, sa = _hgrn(proj, lw["lb"], lw["norm_a"], states["hgrn"], tb=tb, t_valid=t_valid)
    ob = attn_fn(proj)
    oc, sc = _gdn(proj, lw["conv_w"], states["conv"], lw["alog"], lw["dtb"], lw["norm_c"], states["gdn"],
                  tb=min(tb, 2 * GDN_CHUNK), t_valid=t_valid)
    h = _merge(oa.reshape(m, BR_W), ob.reshape(m, BR_W), oc.reshape(m, BR_W), proj.reshape(m, PROJ_W),
               lw["w_cat"], lw["m_post"], h, layer, tm=min(tm, 512))
    y = _ffn(h, lw["f2_pre"], lw["f2_post"], *lw["f2_w"], layer, tm=tm, tf=tf)
    return y.reshape(bsz, t, D_MODEL), proj, sa, sc, (k_all, v_all)


def kernel(x_prompt, x_sample, cache_k, cache_v, state_hgrn, state_gdn, state_conv, page_table,
           ffn1_norm_pre, ffn1_norm_post, ffn1_w_gate, ffn1_w_up, ffn1_w_down,
           mix_norm_pre, mix_norm_post, w_in, hgrn_lb_raw, hgrn_norm,
           gdn_conv_w, gdn_a_log, gdn_dt_bias, gdn_norm, sb_bias, w_branch, w_out,
           ffn2_norm_pre, ffn2_norm_post, ffn2_w_gate, ffn2_w_up, ffn2_w_down):
    depth = w_in.shape[0]
    bp, seq, _ = x_prompt.shape
    db, dec_seq, _ = x_sample.shape
    tpad = SUBLANES * pl.cdiv(dec_seq, SUBLANES)

    lbs = jnp.cumsum(jax.nn.softmax(hgrn_lb_raw.astype(F32), axis=0), axis=0)
    lbs = lbs - lbs[:1]

    w_in_r = _regroup_w_in(w_in)
    w_cat = jnp.concatenate([w_branch.reshape(depth, 3 * BR_W, D_MODEL), w_out], axis=1).astype(BF16)
    bf = lambda w: w.astype(BF16)
    f1 = (bf(ffn1_w_gate), bf(ffn1_w_up), bf(ffn1_w_down))
    f2 = (bf(ffn2_w_gate), bf(ffn2_w_up), bf(ffn2_w_down))
    cache_k2 = cache_k.reshape(depth, cache_k.shape[1], PAGE * N_HEADS, HEAD_DIM)
    cache_v2 = cache_v.reshape(depth, cache_v.shape[1], PAGE * N_HEADS, HEAD_DIM)
    alog_p, dtb_p = _pad_lanes(gdn_a_log), _pad_lanes(gdn_dt_bias)
    bias_prompt = jnp.broadcast_to(sb_bias.astype(F32)[:, :, None, None], (depth, N_HEADS, 1, LANES))
    bias_sample = _pad_lanes(jnp.repeat(sb_bias.astype(F32), tpad, axis=1))

    row = lambda v: v.reshape(1, -1)
    zero_states = dict(hgrn=jnp.zeros((bp, N_HEADS, HEAD_DIM, HEAD_DIM), F32),
                       gdn=jnp.zeros((bp, N_HEADS, HEAD_DIM, HEAD_DIM), F32),
                       conv=jnp.zeros((bp, CONV_K - 1, 3 * BR_W), F32))

    yp = x_prompt
    ys = jnp.pad(x_sample, ((0, 0), (0, tpad - dec_seq), (0, 0)))
    kv_p = tuple(jnp.zeros((depth, bp * seq * N_HEADS, HEAD_DIM), F32) for _ in range(2))
    kv_s = tuple(jnp.zeros((depth, db * tpad * N_HEADS, HEAD_DIM), F32) for _ in range(2))
    outs = [[] for _ in range(6)]
    for l in range(depth):
        lw = dict(
            f1_pre=row(ffn1_norm_pre[l]), f1_post=row(ffn1_norm_post[l]), f1_w=f1,
            m_pre=row(mix_norm_pre[l]), m_post=row(mix_norm_post[l]), w_in=w_in_r, lb=row(lbs[l]),
            norm_a=row(hgrn_norm[l]), conv_w=gdn_conv_w[l], alog=alog_p[l:l + 1], dtb=dtb_p[l:l + 1],
            norm_c=row(gdn_norm[l]), w_cat=w_cat,
            f2_pre=row(ffn2_norm_pre[l]), f2_post=row(ffn2_norm_post[l]), f2_w=f2,
        )
        yp, pp, sa, sc, kv_p = _stream_layer(
            yp, lw, zero_states, kv_p, l, lambda p: _sb_prompt(p, bias_prompt[l]),
            tm=1024, tf=512, tn=512, tb=256, t_valid=256)
        sample_states = dict(hgrn=state_hgrn[l], gdn=state_gdn[l], conv=state_conv[l])
        ys, ps, sas, scs, kv_s = _stream_layer(
            ys, lw, sample_states, kv_s, l,
            lambda p: _sb_sample(p, cache_k2, cache_v2, page_table, bias_sample[l:l + 1], l, t_valid=dec_seq),
            tm=db * tpad, tf=512, tn=512, tb=tpad, t_valid=dec_seq)

        conv_cols = slice(CQ_BLK * LANES, CQ_BLK * LANES + 3 * BR_W)
        keep = CONV_K - 1
        vals = (sa, sas, sc, scs, pp[:, seq - keep:seq, conv_cols], ps[:, dec_seq - keep:dec_seq, conv_cols])
        for acc, v in zip(outs, vals):
            acc.append(v)

    stacked = [jnp.stack(v) for v in outs]
    kv_prompt = [a.reshape(depth, bp, seq, N_HEADS, HEAD_DIM) for a in kv_p]
    kv_sample = [a.reshape(depth, db, tpad, N_HEADS, HEAD_DIM)[:, :, :dec_seq] for a in kv_s]
    return (yp, ys[:, :dec_seq], *stacked, *kv_prompt, *kv_sample)
```

```python
import functools

import jax
import jax.numpy as jnp
from jax import lax
from jax.experimental import pallas as pl
from jax.experimental.pallas import tpu as pltpu

F32 = jnp.float32
BF16 = jnp.bfloat16

D_MODEL = 2048
D_FF = 5632
N_HEADS = 8
HEAD_DIM = 128
BR_W = N_HEADS * HEAD_DIM
CONV_K = 4
RMS_EPS = 1e-6
LANES = 128
SUBLANES = 8
VMEM_LIMIT = 56 * 1024 * 1024

GATE_BLK = 0
AQ_BLK, AF_BLK, AI_BLK, AG_BLK = 48, 56, 64, 72
BQ_BLK, BK_BLK, BV_BLK = 80, 88, 96
CQ_BLK, CK_BLK, CV_BLK = 104, 112, 120
CZ_BLK = 128
AB_BLK = 136
PROJ_W = 140 * LANES

HGRN_CHUNK = 32
GDN_CHUNK = 64
SB_BLOCK = 256
SB_HEADS_PER_STEP = 4
PAGE = 128
PAGES_PER_STEP = 8
FFN_MERGE_ROWS = 512


def _cparams(semantics):
    return pltpu.CompilerParams(dimension_semantics=semantics, vmem_limit_bytes=VMEM_LIMIT)


def _rms(x, g):
    ms = jnp.mean(x * x, axis=-1, keepdims=True)
    return x * lax.rsqrt(ms + RMS_EPS) * g


def _silu(x):
    return x * jax.nn.sigmoid(x)


def _dot(a, b):
    return jnp.dot(a.astype(BF16), b.astype(BF16), preferred_element_type=F32)


def _dot_nt(a, b):
    return lax.dot_general(a.astype(BF16), b.astype(BF16), (((1,), (1,)), ((), ())),
                           preferred_element_type=F32)


def _dot_tn(a, b):
    return lax.dot_general(a.astype(BF16), b.astype(BF16), (((0,), (0,)), ((), ())),
                           preferred_element_type=F32)


def _split2(x):
    hi = x.astype(BF16)
    return hi, (x - hi.astype(F32)).astype(BF16)


def _sum_lhs(m_bf16, x):
    hi, lo = _split2(x)
    d = functools.partial(jnp.dot, preferred_element_type=F32)
    return d(m_bf16, hi) + d(m_bf16, lo)


def _sum_rhs(x, m_twice_bf16):
    return jnp.dot(jnp.concatenate(_split2(x), axis=1), m_twice_bf16, preferred_element_type=F32)


def _iota2(shape, dim):
    return lax.broadcasted_iota(jnp.int32, shape, dim)


def _head(h):
    return slice(h * HEAD_DIM, (h + 1) * HEAD_DIM)


def _ffn_body(x_ref, pre_ref, post_ref, wg_ref, wu_ref, wd_ref, o_ref, xn_ref, acc_ref):
    j = pl.program_id(1)

    @pl.when(j == 0)
    def _():
        xn_ref[...] = _rms(x_ref[...], pre_ref[...]).astype(BF16)
        acc_ref[...] = jnp.zeros_like(acc_ref)

    xn = xn_ref[...]
    g = jnp.dot(xn, wg_ref[...], preferred_element_type=F32)
    u = jnp.dot(xn, wu_ref[...], preferred_element_type=F32)
    acc_ref[...] += _dot(_silu(g) * u, wd_ref[...])

    @pl.when(j == pl.num_programs(1) - 1)
    def _():
        o_ref[...] = x_ref[...] + 0.5 * _rms(acc_ref[...], post_ref[...])


def _ffn(x, pre, post, wg, wu, wd, layer, *, tm, tf):
    m = x.shape[0]
    return pl.pallas_call(
        _ffn_body,
        grid=(m // tm, D_FF // tf),
        in_specs=[
            pl.BlockSpec((tm, D_MODEL), lambda i, j: (i, 0)),
            pl.BlockSpec((1, D_MODEL), lambda i, j: (0, 0)),
            pl.BlockSpec((1, D_MODEL), lambda i, j: (0, 0)),
            pl.BlockSpec((None, D_MODEL, tf), lambda i, j: (layer, 0, j)),
            pl.BlockSpec((None, D_MODEL, tf), lambda i, j: (layer, 0, j)),
            pl.BlockSpec((None, tf, D_MODEL), lambda i, j: (layer, j, 0)),
        ],
        out_specs=pl.BlockSpec((tm, D_MODEL), lambda i, j: (i, 0)),
        out_shape=jax.ShapeDtypeStruct((m, D_MODEL), F32),
        scratch_shapes=[pltpu.VMEM((tm, D_MODEL), BF16), pltpu.VMEM((tm, D_MODEL), F32)],
        compiler_params=_cparams(("parallel", "arbitrary")),
        name="ffn",
    )(x, pre, post, wg, wu, wd)


def _inproj_body(x_ref, pre_ref, w_ref, kin_ref, vin_ref, o_ref, k_ref, v_ref, xn_ref, *, tm, tn):
    del kin_ref, vin_ref
    j = pl.program_id(1)

    @pl.when(j == 0)
    def _():
        xn_ref[...] = _rms(x_ref[...], pre_ref[...]).astype(BF16)

    o_ref[...] = _dot_nt(xn_ref[...], w_ref[...])

    heads_per_tile = tn // HEAD_DIM
    for dst_ref, blk in ((k_ref, BK_BLK), (v_ref, BV_BLK)):
        for q in range(N_HEADS // heads_per_tile):
            @pl.when(j == blk // heads_per_tile + q)
            def _(dst_ref=dst_ref, q=q):
                for hh in range(heads_per_tile):
                    dst_ref[pl.ds(q * heads_per_tile + hh, tm, stride=N_HEADS), :] = o_ref[:, _head(hh)]


def _inproj(x, pre, w, k_all, v_all, layer, *, tm, tn):
    m = x.shape[0]
    assert tn % HEAD_DIM == 0 and BR_W % tn == 0 and (BK_BLK * LANES) % tn == 0 and (BV_BLK * LANES) % tn == 0
    kv_spec = pl.BlockSpec((None, tm * N_HEADS, HEAD_DIM), lambda i, j: (layer, i, 0))
    return pl.pallas_call(
        functools.partial(_inproj_body, tm=tm, tn=tn),
        grid=(m // tm, PROJ_W // tn),
        in_specs=[
            pl.BlockSpec((tm, D_MODEL), lambda i, j: (i, 0)),
            pl.BlockSpec((1, D_MODEL), lambda i, j: (0, 0)),
            pl.BlockSpec((None, tn, D_MODEL), lambda i, j: (layer, j, 0)),
            pl.BlockSpec(memory_space=pl.ANY),
            pl.BlockSpec(memory_space=pl.ANY),
        ],
        out_specs=[pl.BlockSpec((tm, tn), lambda i, j: (i, j)), kv_spec, kv_spec],
        out_shape=[jax.ShapeDtypeStruct((m, PROJ_W), F32),
                   jax.ShapeDtypeStruct(k_all.shape, F32), jax.ShapeDtypeStruct(v_all.shape, F32)],
        input_output_aliases={3: 1, 4: 2},
        scratch_shapes=[pltpu.VMEM((tm, D_MODEL), BF16)],
        compiler_params=_cparams(("parallel", "arbitrary")),
        name="inproj",
    )(x, pre, w, k_all, v_all)


def _regroup_w_in(w_in):
    w_t = jnp.swapaxes(w_in, 1, 2)
    z_end = AB_BLK * LANES - 3 * D_MODEL
    ab_end = z_end + 2 * N_HEADS
    assert w_t.shape[1] == AB_BLK * LANES + 2 * N_HEADS
    pad = jnp.zeros((w_t.shape[0], PROJ_W - w_t.shape[1], w_t.shape[2]), w_t.dtype)
    return jnp.concatenate([w_t[:, ab_end:], w_t[:, :z_end], w_t[:, z_end:ab_end], pad], axis=1).astype(BF16)


def _merge_body(oa_ref, ob_ref, oc_ref, gate_ref, w_ref, post_ref, h_ref, out_ref, acc_ref, y_ref):
    j = pl.program_id(1)

    def branch(o_ref, first):
        m = jnp.dot(o_ref[...], w_ref[...], preferred_element_type=F32) * jax.nn.sigmoid(gate_ref[...])
        if first:
            acc_ref[...] = m
        else:
            acc_ref[...] += m

    pl.when(j == 0)(lambda: branch(oa_ref, True))
    pl.when(j == 1)(lambda: branch(ob_ref, False))
    pl.when(j == 2)(lambda: branch(oc_ref, False))

    @pl.when(j == 3)
    def _():
        y_ref[...] = _dot(acc_ref[:, :BR_W], w_ref[...])

    @pl.when(j == 4)
    def _():
        y = y_ref[...] + _dot(acc_ref[:, BR_W:], w_ref[...])
        out_ref[...] = h_ref[...] + _rms(y, post_ref[...])


def _merge(oa, ob, oc, proj, w_cat, post, h, layer, *, tm):
    m = h.shape[0]
    o_spec = pl.BlockSpec((tm, BR_W), lambda i, j: (i, 0))
    return pl.pallas_call(
        _merge_body,
        grid=(m // tm, 5),
        in_specs=[
            o_spec, o_spec, o_spec,
            pl.BlockSpec((tm, D_MODEL), lambda i, j: (i, jnp.minimum(j, 2))),
            pl.BlockSpec((None, BR_W, D_MODEL), lambda i, j: (layer, j, 0)),
            pl.BlockSpec((1, D_MODEL), lambda i, j: (0, 0)),
            pl.BlockSpec((tm, D_MODEL), lambda i, j: (i, 0)),
        ],
        out_specs=pl.BlockSpec((tm, D_MODEL), lambda i, j: (i, 0)),
        out_shape=jax.ShapeDtypeStruct((m, D_MODEL), F32),
        scratch_shapes=[pltpu.VMEM((tm, D_MODEL), F32), pltpu.VMEM((tm, D_MODEL), F32)],
        compiler_params=_cparams(("parallel", "arbitrary")),
        name="merge",
    )(oa, ob, oc, proj, w_cat, post, h)


def _hgrn_body(q_ref, f_ref, i_ref, g_ref, lb_ref, norm_ref, s0_ref, o_ref, sout_ref, st_ref,
               *, tb, chunk, t_valid):
    tt = pl.program_id(1)

    @pl.when(tt == 0)
    def _():
        for h in range(N_HEADS):
            st_ref[h] = s0_ref[h].T

    lb = lb_ref[...]
    one_m_lb = 1.0 - lb
    norm = norm_ref[...]
    incl = _iota2((chunk, chunk), 0) >= _iota2((chunk, chunk), 1)
    tril = incl.astype(BF16)
    scale = HEAD_DIM ** -0.5
    states = [st_ref[h] for h in range(N_HEADS)]

    for ci in range(tb // chunk):
        rows = slice(ci * chunk, (ci + 1) * chunk)
        sig = jax.nn.sigmoid(f_ref[rows, :])
        log_f = jnp.log(lb + one_m_lb * sig)
        k = one_m_lb * (1.0 - sig)
        if t_valid < tb:
            ok = (_iota2((chunk, 1), 0) + ci * chunk) < t_valid
            log_f = jnp.where(ok, log_f, 0.0)
            k = jnp.where(ok, k, 0.0)
        b = _sum_lhs(tril, log_f)
        qh = _silu(q_ref[rows, :]) * scale * jnp.exp(b)
        kh = k * jnp.exp(-b)
        b_last = b[chunk - 1:chunk, :]
        kd = k * jnp.exp(b_last - b)
        dec = jnp.exp(b_last)
        v = i_ref[rows, :].astype(BF16)
        gate = _silu(g_ref[rows, :])
        qh, kh, kd = qh.astype(BF16), kh.astype(BF16), kd.astype(BF16)
        heads = range(N_HEADS)
        attn = [_dot_nt(qh[:, _head(h)], kh[:, _head(h)]) for h in heads]
        inter = [_dot_nt(qh[:, _head(h)], states[h]) for h in heads]
        kv = [_dot_tn(v[:, _head(h)], kd[:, _head(h)]) for h in heads]
        o = [inter[h] + _dot(jnp.where(incl, attn[h], 0.0), v[:, _head(h)]) for h in heads]
        states = [states[h] * dec[:, _head(h)] + kv[h] for h in heads]
        for h in heads:
            o_ref[rows, _head(h)] = (_rms(o[h], norm) * gate[:, _head(h)]).astype(o_ref.dtype)

    for h in range(N_HEADS):
        st_ref[h] = states[h]

    @pl.when(tt == pl.num_programs(1) - 1)
    def _():
        for h in range(N_HEADS):
            sout_ref[h] = states[h].T


def _hgrn(proj, lb, norm, s0, *, tb, t_valid):
    bsz, t, _ = proj.shape
    chunk = min(HGRN_CHUNK, tb)

    def pspec(blk):
        return pl.BlockSpec((None, tb, BR_W), lambda b, tt: (b, tt, blk // N_HEADS))

    st_spec = pl.BlockSpec((None, N_HEADS, HEAD_DIM, HEAD_DIM), lambda b, tt: (b, 0, 0, 0))
    return pl.pallas_call(
        functools.partial(_hgrn_body, tb=tb, chunk=chunk, t_valid=t_valid),
        grid=(bsz, t // tb),
        in_specs=[
            pspec(AQ_BLK), pspec(AF_BLK), pspec(AI_BLK), pspec(AG_BLK),
            pl.BlockSpec((1, BR_W), lambda b, tt: (0, 0)),
            pl.BlockSpec((1, LANES), lambda b, tt: (0, 0)),
            st_spec,
        ],
        out_specs=[
            pl.BlockSpec((None, tb, BR_W), lambda b, tt: (b, tt, 0)),
            st_spec,
        ],
        out_shape=[
            jax.ShapeDtypeStruct((bsz, t, BR_W), BF16),
            jax.ShapeDtypeStruct((bsz, N_HEADS, HEAD_DIM, HEAD_DIM), F32),
        ],
        scratch_shapes=[pltpu.VMEM((N_HEADS, HEAD_DIM, HEAD_DIM), F32)],
        compiler_params=_cparams(("parallel", "arbitrary")),
        name="hgrn",
    )(proj, proj, proj, proj, lb, norm, s0)


def _gdn_body(q_ref, k_ref, v_ref, z_ref, ab_ref, w_ref, c0_ref, alog_ref, dt_ref, norm_ref, s0_ref,
              o_ref, sout_ref, s_ref, xq_ref, xk_ref, xv_ref, *, tb, chunk, t_valid):
    tt = pl.program_id(1)
    pad = SUBLANES
    keep = CONV_K - 1
    xrefs = (xq_ref, xk_ref, xv_ref)

    @pl.when(tt == 0)
    def _():
        s_ref[...] = s0_ref[...]
        for n, x_ref in enumerate(xrefs):
            x_ref[pad - keep:pad, :] = c0_ref[:, n * BR_W:(n + 1) * BR_W]

    for n, (x_ref, src_ref) in enumerate(zip(xrefs, (q_ref, k_ref, v_ref))):
        w = w_ref[:, n * BR_W:(n + 1) * BR_W]
        x_ref[pad:pad + tb, :] = src_ref[...]
        y = x_ref[pad:pad + tb, :] * w[keep:keep + 1, :]
        for j in range(keep):
            y = y + x_ref[pad - keep + j:pad - keep + j + tb, :] * w[j:j + 1, :]
        x_ref[pad - keep:pad, :] = x_ref[pad + tb - keep:pad + tb, :]
        x_ref[pad:pad + tb, :] = _silu(y)

    neg_a = -jnp.exp(alog_ref[...])
    norm = norm_ref[...]
    row = _iota2((chunk, chunk), 0)
    col = _iota2((chunk, chunk), 1)
    incl = row >= col
    strict = row > col
    tril = incl.astype(BF16)
    eye = (row == col).astype(F32)
    states = [s_ref[h] for h in range(N_HEADS)]
    n_chunks = tb // chunk
    chains = [(ci, h) for ci in range(n_chunks) for h in range(N_HEADS)]

    zg, c = [], {}
    for ci in range(n_chunks):
        rows = slice(ci * chunk, (ci + 1) * chunk)
        xrows = slice(pad + ci * chunk, pad + (ci + 1) * chunk)
        ab = ab_ref[rows, :]
        log_a_all = neg_a * jax.nn.softplus(ab + dt_ref[...])
        beta_all = jax.nn.sigmoid(ab)
        if t_valid < tb:
            ok = (_iota2((chunk, 1), 0) + ci * chunk) < t_valid
            log_a_all = jnp.where(ok, log_a_all, 0.0)
            beta_all = jnp.where(ok, beta_all, 0.0)
        g_all = _sum_lhs(tril, log_a_all)
        g_rows = g_all.T
        eg_all = jnp.exp(g_all)
        zg.append(_silu(z_ref[rows, :]))
        for h in range(N_HEADS):
            cs = _head(h)
            qc, kc, vc = xq_ref[xrows, cs], xk_ref[xrows, cs], xv_ref[xrows, cs]
            qn = qc * lax.rsqrt(jnp.sum(qc * qc, axis=-1, keepdims=True) + 1e-6) * (HEAD_DIM ** -0.5)
            kn = kc * lax.rsqrt(jnp.sum(kc * kc, axis=-1, keepdims=True) + 1e-6)
            beta = beta_all[:, N_HEADS + h:N_HEADS + h + 1]
            g = g_all[:, h:h + 1]
            eg = eg_all[:, h:h + 1]
            g_last = g[chunk - 1:chunk, :]
            c[ci, h] = dict(
                qn=qn.astype(BF16), kn=kn.astype(BF16), beta=beta,
                gam=jnp.where(incl, jnp.exp(g - g_rows[h:h + 1, :]), 0.0),
                rhs=jnp.concatenate([vc * beta, kn * (beta * eg)], axis=1).astype(BF16),
                q_eg=(qn * eg).astype(BF16), k_dec=(kn * jnp.exp(g_last - g)).astype(BF16),
                s_dec=jnp.exp(g_last))

    kk = {ch: _dot_nt(c[ch]["kn"], c[ch]["kn"]) for ch in chains}
    qk = {ch: _dot_nt(c[ch]["qn"], c[ch]["kn"]) for ch in chains}
    a = {ch: jnp.where(strict, kk[ch] * c[ch]["gam"] * c[ch]["beta"], 0.0) for ch in chains}
    aqk = {ch: (qk[ch] * c[ch]["gam"]).astype(BF16) for ch in chains}
    t_inv = {ch: eye - jnp.where(row // 2 == col // 2, a[ch], 0.0) for ch in chains}
    size = 2
    while size < chunk:
        below = (row // (2 * size) == col // (2 * size)) & (row // size > col // size)
        at = {ch: _dot(jnp.where(below, a[ch], 0.0), t_inv[ch]) for ch in chains}
        t_inv = {ch: t_inv[ch] - _dot(t_inv[ch], at[ch]) for ch in chains}
        size *= 2
    uw = {ch: _dot(t_inv[ch], c[ch]["rhs"]) for ch in chains}

    for ci in range(n_chunks):
        rows = slice(ci * chunk, (ci + 1) * chunk)
        heads = range(N_HEADS)
        ws = [_dot(jnp.concatenate([uw[ci, h][:, HEAD_DIM:].astype(BF16), c[ci, h]["q_eg"]], axis=0), states[h])
              for h in heads]
        v_new = [(uw[ci, h][:, :HEAD_DIM] - ws[h][:chunk]).astype(BF16) for h in heads]
        o = [ws[h][chunk:] + _dot(aqk[ci, h], v_new[h]) for h in heads]
        states = [states[h] * c[ci, h]["s_dec"] + _dot_tn(c[ci, h]["k_dec"], v_new[h]) for h in heads]
        for h in heads:
            o_ref[rows, _head(h)] = (_rms(o[h], norm) * zg[ci][:, _head(h)]).astype(o_ref.dtype)

    for h in range(N_HEADS):
        s_ref[h] = states[h]

    @pl.when(tt == pl.num_programs(1) - 1)
    def _():
        for h in range(N_HEADS):
            sout_ref[h] = states[h]


def _gdn(proj, conv_w, conv0, alog, dtb, norm, s0, *, tb, t_valid):
    bsz, t, _ = proj.shape
    chunk = min(GDN_CHUNK, tb)
    keep = CONV_K - 1

    def pspec(blk):
        return pl.BlockSpec((None, tb, BR_W), lambda b, tt: (b, tt, blk // N_HEADS))

    row_spec = pl.BlockSpec((1, LANES), lambda b, tt: (0, 0))
    st_spec = pl.BlockSpec((None, N_HEADS, HEAD_DIM, HEAD_DIM), lambda b, tt: (b, 0, 0, 0))
    return pl.pallas_call(
        functools.partial(_gdn_body, tb=tb, chunk=chunk, t_valid=t_valid),
        grid=(bsz, t // tb),
        in_specs=[
            pspec(CQ_BLK), pspec(CK_BLK), pspec(CV_BLK), pspec(CZ_BLK),
            pl.BlockSpec((None, tb, LANES), lambda b, tt: (b, tt, AB_BLK)),
            pl.BlockSpec((CONV_K, 3 * BR_W), lambda b, tt: (0, 0)),
            pl.BlockSpec((None, keep, 3 * BR_W), lambda b, tt: (b, 0, 0)),
            row_spec, row_spec, row_spec,
            st_spec,
        ],
        out_specs=[
            pl.BlockSpec((None, tb, BR_W), lambda b, tt: (b, tt, 0)),
            st_spec,
        ],
        out_shape=[
            jax.ShapeDtypeStruct((bsz, t, BR_W), BF16),
            jax.ShapeDtypeStruct((bsz, N_HEADS, HEAD_DIM, HEAD_DIM), F32),
        ],
        scratch_shapes=[pltpu.VMEM((N_HEADS, HEAD_DIM, HEAD_DIM), F32)]
        + [pltpu.VMEM((tb + SUBLANES, BR_W), F32)] * 3,
        compiler_params=_cparams(("parallel", "arbitrary")),
        name="gdn",
    )(proj, proj, proj, proj, proj, conv_w, conv0, alog, dtb, norm, s0)


LOG2E = 1.4426950408889634


def _log2_sigmoid_pair(z2):
    soft = jnp.log2(1.0 + jnp.exp2(-jnp.abs(z2)))
    ls = jnp.minimum(z2, 0.0) - soft
    return ls, ls - z2


def _sbp_body(q_ref, k_ref, v_ref, bias_ref, o_ref, acc_ref, run_ref, *, blk, n_hd):
    qi = pl.program_id(2)
    heads = range(n_hd)
    q = [q_ref[:, _head(h)].astype(BF16) for h in heads]
    bias = [bias_ref[h, 0:1, 0:1] * LOG2E for h in heads]
    scale = HEAD_DIM ** -0.5 * LOG2E
    row = _iota2((blk, blk), 0)
    col = _iota2((blk, blk), 1)
    newer = (row > col).astype(BF16)
    newer = jnp.concatenate([newer, newer], axis=0)
    earlier = col < row

    def key_block(kb, diagonal):
        rows = pl.ds(pl.multiple_of(kb * blk, blk), blk)
        zs = [_dot_nt(q[h], k_ref[rows, _head(h)]) * scale + bias[h] for h in heads]
        pairs = [_log2_sigmoid_pair(z) for z in zs]
        lfs = [jnp.where(earlier, lf, 0.0) if diagonal else lf for _, lf in pairs]
        afters = [_sum_rhs(lf, newer) for lf in lfs]
        for h in heads:
            w = jnp.exp2(pairs[h][0] + afters[h] + run_ref[h])
            if diagonal:
                w = jnp.where(earlier, w, 0.0)
            acc_ref[:, _head(h)] += _dot(w, v_ref[rows, _head(h)])
            run_ref[h] += afters[h][:, 0:1] + lfs[h][:, 0:1]

    acc_ref[...] = jnp.zeros_like(acc_ref)
    run_ref[...] = jnp.zeros_like(run_ref)
    key_block(qi, True)

    def older(i, carry):
        key_block(qi - i, False)
        return carry

    lax.fori_loop(1, qi + 1, older, 0)
    o_ref[...] = acc_ref[...].astype(o_ref.dtype)


def _sb_prompt(proj, bias_rows):
    bsz, t, _ = proj.shape
    blk = min(SB_BLOCK, t)
    n_hd = SB_HEADS_PER_STEP
    wide = n_hd * HEAD_DIM
    kv_spec = lambda base: pl.BlockSpec((None, t, wide), lambda b, hp, qi: (b, 0, base // n_hd + hp))
    return pl.pallas_call(
        functools.partial(_sbp_body, blk=blk, n_hd=n_hd),
        grid=(bsz, N_HEADS // n_hd, t // blk),
        in_specs=[
            pl.BlockSpec((None, blk, wide), lambda b, hp, qi: (b, qi, BQ_BLK // n_hd + hp)),
            kv_spec(BK_BLK), kv_spec(BV_BLK),
            pl.BlockSpec((n_hd, 1, LANES), lambda b, hp, qi: (hp, 0, 0)),
        ],
        out_specs=pl.BlockSpec((None, blk, wide), lambda b, hp, qi: (b, qi, hp)),
        out_shape=jax.ShapeDtypeStruct((bsz, t, BR_W), BF16),
        scratch_shapes=[pltpu.VMEM((blk, wide), F32), pltpu.VMEM((n_hd, blk, 1), F32)],
        compiler_params=_cparams(("parallel", "parallel", "arbitrary")),
        name="sb_prompt",
    )(proj, proj, proj, bias_rows)


def _sbs_body(pt_ref, q_ref, kn_ref, vn_ref, *rest, tpad, t_valid, n_pg):
    kp_refs, vp_refs = rest[:n_pg], rest[n_pg:2 * n_pg]
    bias_ref, o_ref, qbd_ref, acc_ref, run_ref, knew_ref, vnew_ref = rest[2 * n_pg:]
    j = pl.program_id(1)
    scale = HEAD_DIM ** -0.5 * LOG2E
    newer = (_iota2((PAGE, PAGE), 1) > _iota2((PAGE, PAGE), 0)).astype(BF16)

    def pages(ks, vs, run, mask):
        bias = bias_ref[...] * LOG2E
        zs = [_dot_nt(k, qbd_ref[...]) * scale + bias for k in ks]
        pairs = [_log2_sigmoid_pair(z) for z in zs]
        lfs = [lf if mask is None else jnp.where(mask, lf, 0.0) for _, lf in pairs]
        afters = [_sum_lhs(newer, lf) for lf in lfs]
        total = None
        for (ls, _), lf, after, v in zip(pairs, lfs, afters, vs):
            w = jnp.exp2(ls + after + run)
            if mask is not None:
                w = jnp.where(mask, w, 0.0)
            part = _dot_tn(w, v)
            total = part if total is None else total + part
            run = run + after[0:1, :] + lf[0:1, :]
        return total, run

    def heads_on_lanes(ref):
        return jnp.concatenate([ref[pl.ds(h, PAGE, stride=N_HEADS), :] for h in range(N_HEADS)],
                               axis=1).astype(BF16)

    @pl.when(j == 0)
    def _():
        q = q_ref[...]
        tiled = jnp.concatenate([q] * (LANES // tpad), axis=0)
        r = _iota2((LANES, BR_W), 0)
        c = _iota2((LANES, BR_W), 1)
        qbd_ref[...] = jnp.where(r // tpad == c // HEAD_DIM, tiled, 0.0).astype(BF16)
        knew_ref[...] = jnp.zeros_like(knew_ref)
        vnew_ref[...] = jnp.zeros_like(vnew_ref)
        knew_ref[0:tpad, :] = kn_ref[...].astype(BF16)
        vnew_ref[0:tpad, :] = vn_ref[...].astype(BF16)
        row = _iota2((PAGE, LANES), 0)
        col = _iota2((PAGE, LANES), 1)
        mask = (row < col % tpad) & (row < t_valid)
        acc, run = pages([knew_ref[...]], [vnew_ref[...]], jnp.zeros((1, LANES), F32), mask)
        acc_ref[...] = acc
        run_ref[...] = run

    @pl.when(j > 0)
    def _():
        total, run = pages([heads_on_lanes(r) for r in kp_refs], [heads_on_lanes(r) for r in vp_refs],
                           run_ref[...], None)
        acc_ref[...] += total
        run_ref[...] = run

    @pl.when(j == pl.num_programs(1) - 1)
    def _():
        for hh in range(N_HEADS):
            o_ref[:, _head(hh)] = acc_ref[hh * tpad:(hh + 1) * tpad, _head(hh)].astype(o_ref.dtype)


def _sb_sample(proj, cache_k, cache_v, page_table, bias_row, layer, *, t_valid):
    bsz, tpad, _ = proj.shape
    n_pages = page_table.shape[1]
    n_pg = PAGES_PER_STEP
    assert N_HEADS * tpad <= LANES and cache_k.shape[2:] == (PAGE * N_HEADS, HEAD_DIM) and n_pages % n_pg == 0

    def pspec(blk):
        return pl.BlockSpec((None, tpad, BR_W), lambda b, j, pt: (b, 0, blk // N_HEADS))

    def cache_spec(g):
        def index(b, j, pt):
            return (layer, pt[b, n_pages - 1 - (jnp.maximum(j, 1) - 1) * n_pg - g], 0, 0)
        return pl.BlockSpec((None, None, PAGE * N_HEADS, HEAD_DIM), index)

    return pl.pallas_call(
        functools.partial(_sbs_body, tpad=tpad, t_valid=t_valid, n_pg=n_pg),
        grid_spec=pltpu.PrefetchScalarGridSpec(
            num_scalar_prefetch=1,
            grid=(bsz, n_pages // n_pg + 1),
            in_specs=[pspec(BQ_BLK), pspec(BK_BLK), pspec(BV_BLK)]
            + [cache_spec(g) for g in range(n_pg)] * 2
            + [pl.BlockSpec((1, LANES), lambda b, j, pt: (0, 0))],
            out_specs=pl.BlockSpec((None, tpad, BR_W), lambda b, j, pt: (b, 0, 0)),
            scratch_shapes=[
                pltpu.VMEM((LANES, BR_W), BF16),
                pltpu.VMEM((LANES, BR_W), F32),
                pltpu.VMEM((1, LANES), F32),
                pltpu.VMEM((PAGE, BR_W), BF16),
                pltpu.VMEM((PAGE, BR_W), BF16),
            ],
        ),
        out_shape=jax.ShapeDtypeStruct((bsz, tpad, BR_W), BF16),
        compiler_params=_cparams(("parallel", "arbitrary")),
        name="sb_sample",
    )(page_table, proj, proj, proj, *([cache_k] * n_pg), *([cache_v] * n_pg), bias_row)


def _pad_lanes(v):
    return jnp.pad(v.astype(F32), ((0, 0), (0, LANES - v.shape[-1])))


def _stream_layer(x, lw, states, kv, layer, attn_fn, *, tm, tf, tn, tb, t_valid):
    bsz, t, _ = x.shape
    m = bsz * t
    x2 = x.reshape(m, D_MODEL)
    tm_small = min(tm, FFN_MERGE_ROWS)
    h = _ffn(x2, lw["f1_pre"], lw["f1_post"], *lw["f1_w"], layer, tm=tm_small, tf=tf)
    proj, k_all, v_all = _inproj(h, lw["m_pre"], lw["w_in"], kv[0], kv[1], layer, tm=tm, tn=tn)
    proj = proj.reshape(bsz, t, PROJ_W)
    oa, sa = _hgrn(proj, lw["lb"], lw["norm_a"], states["hgrn"], tb=tb, t_valid=t_valid)
    ob = attn_fn(proj)
    oc, sc = _gdn(proj, lw["conv_w"], states["conv"], lw["alog"], lw["dtb"], lw["norm_c"], states["gdn"],
                  tb=min(tb, 2 * GDN_CHUNK), t_valid=t_valid)
    h = _merge(oa.reshape(m, BR_W), ob.reshape(m, BR_W), oc.reshape(m, BR_W), proj.reshape(m, PROJ_W),
               lw["w_cat"], lw["m_post"], h, layer, tm=tm_small)
    y = _ffn(h, lw["f2_pre"], lw["f2_post"], *lw["f2_w"], layer, tm=tm_small, tf=tf)
    return y.reshape(bsz, t, D_MODEL), proj, sa, sc, (k_all, v_all)


def kernel(x_prompt, x_sample, cache_k, cache_v, state_hgrn, state_gdn, state_conv, page_table,
           ffn1_norm_pre, ffn1_norm_post, ffn1_w_gate, ffn1_w_up, ffn1_w_down,
           mix_norm_pre, mix_norm_post, w_in, hgrn_lb_raw, hgrn_norm,
           gdn_conv_w, gdn_a_log, gdn_dt_bias, gdn_norm, sb_bias, w_branch, w_out,
           ffn2_norm_pre, ffn2_norm_post, ffn2_w_gate, ffn2_w_up, ffn2_w_down):
    depth = w_in.shape[0]
    bp, seq, _ = x_prompt.shape
    db, dec_seq, _ = x_sample.shape
    tpad = SUBLANES * pl.cdiv(dec_seq, SUBLANES)

    lbs = jnp.cumsum(jax.nn.softmax(hgrn_lb_raw.astype(F32), axis=0), axis=0)
    lbs = lbs - lbs[:1]

    w_in_r = _regroup_w_in(w_in)
    w_cat = jnp.concatenate([w_branch.reshape(depth, 3 * BR_W, D_MODEL), w_out], axis=1).astype(BF16)
    bf = lambda w: w.astype(BF16)
    f1 = (bf(ffn1_w_gate), bf(ffn1_w_up), bf(ffn1_w_down))
    f2 = (bf(ffn2_w_gate), bf(ffn2_w_up), bf(ffn2_w_down))
    cache_k2 = cache_k.reshape(depth, cache_k.shape[1], PAGE * N_HEADS, HEAD_DIM)
    cache_v2 = cache_v.reshape(depth, cache_v.shape[1], PAGE * N_HEADS, HEAD_DIM)
    alog_p, dtb_p = _pad_lanes(gdn_a_log), _pad_lanes(gdn_dt_bias)
    bias_prompt = jnp.broadcast_to(sb_bias.astype(F32)[:, :, None, None], (depth, N_HEADS, 1, LANES))
    bias_sample = _pad_lanes(jnp.repeat(sb_bias.astype(F32), tpad, axis=1))

    row = lambda v: v.reshape(1, -1)
    zero_states = dict(hgrn=jnp.zeros((bp, N_HEADS, HEAD_DIM, HEAD_DIM), F32),
                       gdn=jnp.zeros((bp, N_HEADS, HEAD_DIM, HEAD_DIM), F32),
                       conv=jnp.zeros((bp, CONV_K - 1, 3 * BR_W), F32))

    yp = x_prompt
    ys = jnp.pad(x_sample, ((0, 0), (0, tpad - dec_seq), (0, 0)))
    kv_p = tuple(jnp.zeros((depth, bp * seq * N_HEADS, HEAD_DIM), F32) for _ in range(2))
    kv_s = tuple(jnp.zeros((depth, db * tpad * N_HEADS, HEAD_DIM), F32) for _ in range(2))
    outs = [[] for _ in range(6)]
    for l in range(depth):
        lw = dict(
            f1_pre=row(ffn1_norm_pre[l]), f1_post=row(ffn1_norm_post[l]), f1_w=f1,
            m_pre=row(mix_norm_pre[l]), m_post=row(mix_norm_post[l]), w_in=w_in_r, lb=row(lbs[l]),
            norm_a=row(hgrn_norm[l]), conv_w=gdn_conv_w[l], alog=alog_p[l:l + 1], dtb=dtb_p[l:l + 1],
            norm_c=row(gdn_norm[l]), w_cat=w_cat,
            f2_pre=row(ffn2_norm_pre[l]), f2_post=row(ffn2_norm_post[l]), f2_w=f2,
        )
        yp, pp, sa, sc, kv_p = _stream_layer(
            yp, lw, zero_states, kv_p, l, lambda p: _sb_prompt(p, bias_prompt[l]),
            tm=1024, tf=512, tn=512, tb=256, t_valid=256)
        sample_states = dict(hgrn=state_hgrn[l], gdn=state_gdn[l], conv=state_conv[l])
        ys, ps, sas, scs, kv_s = _stream_layer(
            ys, lw, sample_states, kv_s, l,
            lambda p: _sb_sample(p, cache_k2, cache_v2, page_table, bias_sample[l:l + 1], l, t_valid=dec_seq),
            tm=db * tpad, tf=512, tn=512, tb=tpad, t_valid=dec_seq)

        conv_cols = slice(CQ_BLK * LANES, CQ_BLK * LANES + 3 * BR_W)
        keep = CONV_K - 1
        vals = (sa, sas, sc, scs, pp[:, seq - keep:seq, conv_cols], ps[:, dec_seq - keep:dec_seq, conv_cols])
        for acc, v in zip(outs, vals):
            acc.append(v)

    stacked = [jnp.stack(v) for v in outs]
    kv_prompt = [a.reshape(depth, bp, seq, N_HEADS, HEAD_DIM) for a in kv_p]
    kv_sample = [a.reshape(depth, db, tpad, N_HEADS, HEAD_DIM)[:, :, :dec_seq] for a in kv_s]
    return (yp, ys[:, :dec_seq], *stacked, *kv_prompt, *kv_sample)
```
